```python
import jax, jax.numpy as jnp
from jax import lax
import numpy as np

D_MODEL = 2048
BATCH = 2
SEQ = 16384
DEPTH = 2

GRID_W = 64
CTX_LEN = 256
HEAD_DIM = 128
N_Q_HEADS = 16
N_KV_HEADS = 4
GROUP = N_Q_HEADS // N_KV_HEADS
WINDOW = 128
BLOCK = 128
ROPE_BASE = 10000.0
ROT_FREQS = HEAD_DIM // 4
CONV_WIDTH = D_MODEL
CONV_K = 3
N_EXPERTS = 32
N_EXPERT_GROUPS = 4
EXPERTS_PER_GROUP = N_EXPERTS // N_EXPERT_GROUPS
TOP_K = 2
EXPERT_FF = 1024
MOE_BLOCK = 256
Q_WIDTH = N_Q_HEADS * HEAD_DIM
KV_WIDTH = N_KV_HEADS * HEAD_DIM
IN_WIDTH = Q_WIDTH + 2 * KV_WIDTH + 3 * CONV_WIDTH + 2 * D_MODEL
ALPHA = (2 * DEPTH) ** 0.25
BETA = (8 * DEPTH) ** -0.25
LN_EPS = 1e-5
NEG_INF = -1e30
ATTN_SCALE = HEAD_DIM ** -0.5

kernel_name = "hybrid_swa_shortconv_groupedmoe_deepnorm_dit"


def layer_norm(x, g, b):
    xf = x.astype(jnp.float32)
    mu = xf.mean(-1, keepdims=True)
    var = jnp.square(xf - mu).mean(-1, keepdims=True)
    return ((xf - mu) * lax.rsqrt(var + LN_EPS) * g + b).astype(x.dtype)


def axial_rope_tables(n):
    t = jnp.arange(n, dtype=jnp.int32)
    row = (t // GRID_W).astype(jnp.float32)
    col = (t % GRID_W).astype(jnp.float32)
    inv_freq = jnp.power(ROPE_BASE, -jnp.arange(ROT_FREQS, dtype=jnp.float32) / ROT_FREQS)
    ang = jnp.stack([row[:, None] * inv_freq, col[:, None] * inv_freq], axis=1)
    return jnp.cos(ang), jnp.sin(ang)


def apply_rope(x, cos, sin):
    b, n, h, d = x.shape
    xr = x.astype(jnp.float32).reshape(b, n, h, 2, 2, ROT_FREQS)
    x1, x2 = xr[..., 0, :], xr[..., 1, :]
    cs, sn = cos[None, :, None], sin[None, :, None]
    out = jnp.stack([x1 * cs - x2 * sn, x2 * cs + x1 * sn], axis=-2)
    return out.reshape(x.shape).astype(x.dtype)


def split_in_proj(p):
    base = Q_WIDTH + 2 * KV_WIDTH
    cuts = [Q_WIDTH, Q_WIDTH + KV_WIDTH, base, base + CONV_WIDTH, base + 2 * CONV_WIDTH,
            base + 3 * CONV_WIDTH, base + 3 * CONV_WIDTH + D_MODEL]
    return jnp.split(p, cuts, axis=-1)


def latent_window_attention(q, k, v, kc, vc, sink):
    b, n = q.shape[0], q.shape[1]
    n_blocks = n // BLOCK
    pad = ((0, 0), (BLOCK, BLOCK), (0, 0), (0, 0))
    kp, vp = jnp.pad(k, pad), jnp.pad(v, pad)
    sink_l = sink.astype(jnp.float32).reshape(N_KV_HEADS, GROUP)
    band = jnp.arange(3 * BLOCK)
    qi = jnp.arange(BLOCK)

    def block(blk):
        q0 = blk * BLOCK
        qb = lax.dynamic_slice_in_dim(q, q0, BLOCK, axis=1).reshape(b, BLOCK, N_KV_HEADS, GROUP, HEAD_DIM)
        kb = lax.dynamic_slice_in_dim(kp, q0, 3 * BLOCK, axis=1)
        vb = lax.dynamic_slice_in_dim(vp, q0, 3 * BLOCK, axis=1)
        qpos = q0 + qi
        kpos = q0 - BLOCK + band
        mask = (jnp.abs(kpos[None, :] - qpos[:, None]) <= WINDOW) & (kpos >= 0)[None, :] & (kpos < n)[None, :]
        s_loc = jnp.einsum('bqhgd,bshd->bhgqs', qb, kb).astype(jnp.float32) * ATTN_SCALE
        s_loc = jnp.where(mask, s_loc, NEG_INF)
        s_ctx = jnp.einsum('bqhgd,bchd->bhgqc', qb, kc).astype(jnp.float32) * ATTN_SCALE
        s_sink = jnp.broadcast_to(sink_l[None, :, :, None, None], (b, N_KV_HEADS, GROUP, BLOCK, 1))
        p = jax.nn.softmax(jnp.concatenate([s_loc, s_ctx, s_sink], axis=-1), axis=-1)
        p_loc = p[..., :3 * BLOCK].astype(v.dtype)
        p_ctx = p[..., 3 * BLOCK:3 * BLOCK + kc.shape[1]].astype(v.dtype)
        o = jnp.einsum('bhgqs,bshd->bqhgd', p_loc, vb) + jnp.einsum('bhgqc,bchd->bqhgd', p_ctx, vc)
        return o.reshape(b, BLOCK, Q_WIDTH)

    o = lax.map(block, jnp.arange(n_blocks))
    return o.transpose(1, 0, 2, 3).reshape(b, n, Q_WIDTH)


def context_attention(qc, kc, vc, sink):
    b, cl = qc.shape[0], qc.shape[1]
    qb = qc.reshape(b, cl, N_KV_HEADS, GROUP, HEAD_DIM)
    s = jnp.einsum('bqhgd,bshd->bhgqs', qb, kc).astype(jnp.float32) * ATTN_SCALE
    s_sink = jnp.broadcast_to(sink.astype(jnp.float32).reshape(N_KV_HEADS, GROUP)[None, :, :, None, None],
                              (b, N_KV_HEADS, GROUP, cl, 1))
    p = jax.nn.softmax(jnp.concatenate([s, s_sink], axis=-1), axis=-1)[..., :cl].astype(vc.dtype)
    return jnp.einsum('bhgqs,bshd->bqhgd', p, vc).reshape(b, cl, Q_WIDTH)


def short_conv(u, w):
    half = CONV_K // 2
    n = u.shape[1]
    up = jnp.pad(u, ((0, 0), (half, half), (0, 0)))
    return sum(up[:, j:j + n] * w[j] for j in range(CONV_K))


def merge_branches(attn_o, b_gate, c_gate, u, g_attn, g_conv, conv_w, w_attn_proj, w_conv_proj, w_out):
    a = attn_o @ w_attn_proj
    s = (b_gate * short_conv(c_gate * u, conv_w)) @ w_conv_proj
    m = jax.nn.sigmoid(g_attn) * a + jax.nn.sigmoid(g_conv) * s
    return m @ w_out


def grouped_moe(h, router_w, router_b, w_gate, w_up, w_down):
    t, d = h.shape
    probs = jax.nn.softmax((h @ router_w).astype(jnp.float32), axis=-1)
    sel = probs + router_b.astype(jnp.float32)
    sel_g = sel.reshape(t, N_EXPERT_GROUPS, EXPERTS_PER_GROUP)
    group_score = lax.top_k(sel_g, TOP_K)[0].sum(-1)
    g_best = jnp.argmax(group_score, axis=-1)
    in_group = jnp.take_along_axis(sel_g, g_best[:, None, None], axis=1)[:, 0]
    local_idx = lax.top_k(in_group, TOP_K)[1]
    expert_idx = g_best[:, None] * EXPERTS_PER_GROUP + local_idx
    wts = jnp.take_along_axis(probs, expert_idx, axis=-1)
    wts = wts / wts.sum(-1, keepdims=True)

    n_assign = t * TOP_K
    e_flat = expert_idx.reshape(-1).astype(jnp.int32)
    tok_flat = jnp.repeat(jnp.arange(t, dtype=jnp.int32), TOP_K)
    w_flat = wts.reshape(-1)
    order = jnp.argsort(e_flat)
    e_s, tok_s, w_s = e_flat[order], tok_flat[order], w_flat[order]
    counts = jnp.zeros((N_EXPERTS,), jnp.int32).at[e_flat].add(1)
    padded = (counts + MOE_BLOCK - 1) // MOE_BLOCK * MOE_BLOCK
    pend = jnp.cumsum(padded)
    pstart = pend - padded
    start = jnp.cumsum(counts) - counts
    dest = pstart[e_s] + jnp.arange(n_assign, dtype=jnp.int32) - start[e_s]
    cap = (n_assign + N_EXPERTS * (MOE_BLOCK - 1) + MOE_BLOCK - 1) // MOE_BLOCK * MOE_BLOCK
    n_blk = cap // MOE_BLOCK
    slot_tok = jnp.full((cap,), t, jnp.int32).at[dest].set(tok_s)
    slot_w = jnp.zeros((cap,), jnp.float32).at[dest].set(w_s)
    blk_expert = jnp.minimum(jnp.searchsorted(pend, jnp.arange(n_blk, dtype=jnp.int32) * MOE_BLOCK, side='right'),
                             N_EXPERTS - 1)
    h_pad = jnp.concatenate([h, jnp.zeros((1, d), h.dtype)], axis=0)
    xs = h_pad[slot_tok].reshape(n_blk, MOE_BLOCK, d)

    def expert_block(args):
        xb, e = args
        return (jax.nn.silu(xb @ w_gate[e]) * (xb @ w_up[e])) @ w_down[e]

    ys = lax.map(expert_block, (xs, blk_expert)).reshape(cap, d)
    out = jnp.zeros((t + 1, d), ys.dtype).at[slot_tok].add(ys * slot_w[:, None].astype(ys.dtype))
    return out[:t]


def setup_inputs(seed: int = 0) -> dict:
    key = jax.random.key(seed)
    ks = jax.random.split(key, 24)

    def nrm(k, shape, scale):
        return jax.random.normal(k, shape, jnp.float32) * scale

    return {
        "x": nrm(ks[0], (BATCH, SEQ, D_MODEL), 1.0),
        "c": nrm(ks[1], (BATCH, D_MODEL), 1.0),
        "ctx": nrm(ks[2], (BATCH, CTX_LEN, D_MODEL), 1.0),
        "c_ctx": nrm(ks[3], (D_MODEL,), 1.0),
        "w_ada": nrm(ks[4], (DEPTH, D_MODEL, 6 * D_MODEL), 0.5 * D_MODEL ** -0.5),
        "b_ada": nrm(ks[5], (DEPTH, 6 * D_MODEL), 0.01),
        "w_in": nrm(ks[6], (DEPTH, D_MODEL, IN_WIDTH), D_MODEL ** -0.5),
        "attn_sink": nrm(ks[7], (DEPTH, N_Q_HEADS), 0.5),
        "conv_w": nrm(ks[8], (DEPTH, CONV_K, CONV_WIDTH), CONV_K ** -0.5),
        "w_attn_proj": nrm(ks[9], (DEPTH, Q_WIDTH, D_MODEL), Q_WIDTH ** -0.5),
        "w_conv_proj": nrm(ks[10], (DEPTH, CONV_WIDTH, D_MODEL), CONV_WIDTH ** -0.5),
        "w_out": nrm(ks[11], (DEPTH, D_MODEL, D_MODEL), BETA * D_MODEL ** -0.5),
        "ln1_g": 1.0 + nrm(ks[12], (DEPTH, D_MODEL), 0.02),
        "ln1_b": nrm(ks[13], (DEPTH, D_MODEL), 0.02),
        "ln2_g": 1.0 + nrm(ks[14], (DEPTH, D_MODEL), 0.02),
        "ln2_b": nrm(ks[15], (DEPTH, D_MODEL), 0.02),
        "router_w": nrm(ks[16], (D_MODEL, N_EXPERTS), D_MODEL ** -0.5),
        "router_b": nrm(ks[17], (N_EXPERTS,), 0.01),
        "w_gate": nrm(ks[18], (DEPTH, N_EXPERTS, D_MODEL, EXPERT_FF), D_MODEL ** -0.5),
        "w_up": nrm(ks[19], (DEPTH, N_EXPERTS, D_MODEL, EXPERT_FF), D_MODEL ** -0.5),
        "w_down": nrm(ks[20], (DEPTH, N_EXPERTS, EXPERT_FF, D_MODEL), BETA * EXPERT_FF ** -0.5),
    }


def reference(x, c, ctx, c_ctx, w_ada, b_ada, w_in, attn_sink, conv_w, w_attn_proj, w_conv_proj, w_out,
              ln1_g, ln1_b, ln2_g, ln2_b, router_w, router_b, w_gate, w_up, w_down):
    b, n, d = x.shape
    cl = ctx.shape[1]
    cos, sin = axial_rope_tables(n)
    silu_c = jax.nn.silu(c)
    silu_cc = jax.nn.silu(c_ctx)
    xc = ctx
    for l in range(DEPTH):
        last = l == DEPTH - 1
        mod = (silu_c @ w_ada[l] + b_ada[l]).reshape(b, 6, d)[:, :, None, :]
        modc = (silu_cc @ w_ada[l] + b_ada[l]).reshape(6, d)

        h = x * (1 + mod[:, 1]) + mod[:, 0]
        hc = xc * (1 + modc[1]) + modc[0]
        q, k, v, cb, cc, cu, ga, gc = split_in_proj(h @ w_in[l])
        q = apply_rope(q.reshape(b, n, N_Q_HEADS, HEAD_DIM), cos, sin)
        k = apply_rope(k.reshape(b, n, N_KV_HEADS, HEAD_DIM), cos, sin)
        v = v.reshape(b, n, N_KV_HEADS, HEAD_DIM)
        if last:
            kc, vc = jnp.split(hc @ w_in[l][:, Q_WIDTH:Q_WIDTH + 2 * KV_WIDTH], 2, axis=-1)
        else:
            qc, kc, vc, cbc, ccc, cuc, gac, gcc = split_in_proj(hc @ w_in[l])
        kc = kc.reshape(b, cl, N_KV_HEADS, HEAD_DIM)
        vc = vc.reshape(b, cl, N_KV_HEADS, HEAD_DIM)

        attn = latent_window_attention(q, k, v, kc, vc, attn_sink[l])
        y = merge_branches(attn, cb, cc, cu, ga, gc, conv_w[l], w_attn_proj[l], w_conv_proj[l], w_out[l])
        x = layer_norm(ALPHA * x + mod[:, 2] * y, ln1_g[l], ln1_b[l])
        if not last:
            attn_c = context_attention(qc, kc, vc, attn_sink[l])
            yc = merge_branches(attn_c, cbc, ccc, cuc, gac, gcc, conv_w[l], w_attn_proj[l], w_conv_proj[l], w_out[l])
            xc = layer_norm(ALPHA * xc + modc[2] * yc, ln1_g[l], ln1_b[l])

        h2 = (x * (1 + mod[:, 4]) + mod[:, 3]).reshape(b * n, d)
        if last:
            f = grouped_moe(h2, router_w, router_b, w_gate[l], w_up[l], w_down[l]).reshape(b, n, d)
            x = layer_norm(ALPHA * x + mod[:, 5] * f, ln2_g[l], ln2_b[l])
        else:
            h2c = (xc * (1 + modc[4]) + modc[3]).reshape(b * cl, d)
            f = grouped_moe(jnp.concatenate([h2, h2c], axis=0), router_w, router_b, w_gate[l], w_up[l], w_down[l])
            x = layer_norm(ALPHA * x + mod[:, 5] * f[:b * n].reshape(b, n, d), ln2_g[l], ln2_b[l])
            xc = layer_norm(ALPHA * xc + modc[5] * f[b * n:].reshape(b, cl, d), ln2_g[l], ln2_b[l])
    return x
```

```python
import functools

import jax
import jax.numpy as jnp
from jax import lax
from jax.experimental import pallas as pl
from jax.experimental.pallas import tpu as pltpu

F32 = jnp.float32
BF16 = jnp.bfloat16
HIGHEST = lax.Precision.HIGHEST

GRID_W = 64
HEAD_DIM = 128
WINDOW = 128
ROPE_BASE = 10000.0
N_EXPERT_GROUPS = 4
TOP_K = 2
LN_EPS = 1e-5
NEG_INF = -1e30

LANE = 128
TQ = WINDOW
TM = 256
TM_IN = 512
TN_IN = 1024
TN_MOD = 1024
MOE_BLOCK = 256
VMEM_LIMIT = 56 * 1024 * 1024

NT_DIMS = (((1,), (1,)), ((), ()))


def _cparams(*sem):
    return pltpu.CompilerParams(dimension_semantics=sem, vmem_limit_bytes=VMEM_LIMIT)


def _layer_norm(z, g, b):
    mu = jnp.mean(z, axis=-1, keepdims=True)
    zc = z - mu
    var = jnp.mean(zc * zc, axis=-1, keepdims=True)
    return zc * lax.rsqrt(var + LN_EPS) * g + b


def _mod_kernel(c_ref, w_ref, b_ref, o_ref):
    cc = c_ref[...]
    s = cc * jax.nn.sigmoid(cc)
    o_ref[...] = jnp.dot(s, w_ref[...], preferred_element_type=F32, precision=HIGHEST) + b_ref[...]


def _mod_table(c_rows, w_ada_l, b_ada_l):
    rows, d = c_rows.shape
    n_out = w_ada_l.shape[1]
    return pl.pallas_call(
        _mod_kernel,
        grid=(n_out // TN_MOD,),
        in_specs=[pl.BlockSpec((rows, d), lambda j: (0, 0)),
                  pl.BlockSpec((d, TN_MOD), lambda j: (0, j)),
                  pl.BlockSpec((1, TN_MOD), lambda j: (0, j))],
        out_specs=pl.BlockSpec((rows, TN_MOD), lambda j: (0, j)),
        out_shape=jax.ShapeDtypeStruct((rows, n_out), F32),
        compiler_params=_cparams("parallel"),
        name="mod_table",
    )(c_rows, w_ada_l, b_ada_l.reshape(1, n_out))


def _inproj_kernel(x_ref, mod_ref, w_ref, rc_ref, rs_ref, o_ref, h_scr, *, n_plain, n_gate, n_q, k_cols, scale):
    j = pl.program_id(1)
    tn = o_ref.shape[1]

    @pl.when(j == 0)
    def _():
        h_scr[...] = (x_ref[...] * (1.0 + mod_ref[0, 1:2, :]) + mod_ref[0, 0:1, :]).astype(BF16)

    acc = jnp.dot(h_scr[...], w_ref[...], preferred_element_type=F32)

    def rope(xh):
        lane = lax.broadcasted_iota(jnp.int32, xh.shape, 1)
        sw = jnp.where((lane & 32) == 0, pltpu.roll(xh, LANE - 32, 1), pltpu.roll(xh, 32, 1))
        return xh * rc_ref[...] + sw * rs_ref[...]

    @pl.when(j < n_plain)
    def _():
        o_ref[...] = acc.astype(BF16)

    @pl.when((j >= n_plain) & (j < n_plain + n_gate))
    def _():
        o_ref[...] = jax.nn.sigmoid(acc).astype(BF16)

    @pl.when((j >= n_plain + n_gate) & (j < n_plain + n_gate + n_q))
    def _():
        for c in range(tn // LANE):
            sl = slice(c * LANE, (c + 1) * LANE)
            o_ref[:, sl] = (rope(acc[:, sl]) * scale).astype(BF16)

    @pl.when(j == n_plain + n_gate + n_q)
    def _():
        for c in range(tn // LANE):
            sl = slice(c * LANE, (c + 1) * LANE)
            if c * LANE < k_cols:
                o_ref[:, sl] = rope(acc[:, sl]).astype(BF16)
            else:
                o_ref[:, sl] = acc[:, sl].astype(BF16)


def _in_proj(xall, mod, w_perm, rope_c, rope_s, *, seg_of_row_tile, d, q_width, kv_width):
    r = xall.shape[0]
    in_width = w_perm.shape[1]
    assert (3 * d) % TN_IN == 0 and (2 * d) % TN_IN == 0 and q_width % TN_IN == 0 and 2 * kv_width == TN_IN
    n_plain, n_gate, n_q = 3 * d // TN_IN, 2 * d // TN_IN, q_width // TN_IN
    assert (n_plain + n_gate + n_q + 1) * TN_IN == in_width
    kern = functools.partial(_inproj_kernel, n_plain=n_plain, n_gate=n_gate, n_q=n_q, k_cols=kv_width,
                             scale=HEAD_DIM ** -0.5)
    return pl.pallas_call(
        kern,
        grid=(r // TM_IN, in_width // TN_IN),
        in_specs=[pl.BlockSpec((TM_IN, d), lambda i, j: (i, 0)),
                  pl.BlockSpec((1, 6, d), lambda i, j: (seg_of_row_tile(i, TM_IN), 0, 0)),
                  pl.BlockSpec((d, TN_IN), lambda i, j: (0, j)),
                  pl.BlockSpec((TM_IN, LANE), lambda i, j: (i, 0)),
                  pl.BlockSpec((TM_IN, LANE), lambda i, j: (i, 0))],
        out_specs=pl.BlockSpec((TM_IN, TN_IN), lambda i, j: (i, j)),
        out_shape=jax.ShapeDtypeStruct((r, in_width), BF16),
        scratch_shapes=[pltpu.VMEM((TM_IN, d), BF16)],
        compiler_params=_cparams("parallel", "arbitrary"),
        name="in_proj",
    )(xall, mod, w_perm, rope_c, rope_s)


def _attn_kernel(sink_ref, q_ref, kp_ref, kc_ref, kn_ref, vp_ref, vc_ref, vn_ref, kx_ref, vx_ref, o_ref, *,
                 nb_seq, n_lat_tiles, n_kv, group):
    i = pl.program_id(0)
    tq = q_ref.shape[0]
    is_lat = i < n_lat_tiles
    p = i % nb_seq
    has_prev = jnp.logical_and(is_lat, p > 0)
    has_next = jnp.logical_and(is_lat, p < nb_seq - 1)
    row = lax.broadcasted_iota(jnp.int32, (tq, tq), 0)
    col = lax.broadcasted_iota(jnp.int32, (tq, tq), 1)
    m_prev = jnp.logical_and(col >= row, has_prev)
    m_cur = jnp.logical_and(col >= 0, is_lat)
    m_next = jnp.logical_and(col <= row, has_next)
    mask1 = jnp.concatenate([m_prev, m_cur, m_next], axis=1)
    mask = jnp.concatenate([mask1] * group, axis=0)

    for h in range(n_kv):
        hs = slice(h * HEAD_DIM, (h + 1) * HEAD_DIM)
        q4 = jnp.concatenate([q_ref[:, (h * group + g) * HEAD_DIM:(h * group + g + 1) * HEAD_DIM]
                              for g in range(group)], axis=0)
        kl = jnp.concatenate([kp_ref[:, hs], kc_ref[:, hs], kn_ref[:, hs]], axis=0)
        vl = jnp.concatenate([vp_ref[:, hs], vc_ref[:, hs], vn_ref[:, hs]], axis=0)
        s_loc = lax.dot_general(q4, kl, NT_DIMS, preferred_element_type=F32)
        s_loc = jnp.where(mask, s_loc, NEG_INF)
        s_ctx = lax.dot_general(q4, kx_ref[:, hs], NT_DIMS, preferred_element_type=F32)
        s_sink = jnp.concatenate([jnp.full((tq, 1), sink_ref[h * group + g], F32) for g in range(group)], axis=0)
        m = jnp.maximum(jnp.maximum(jnp.max(s_loc, axis=-1, keepdims=True),
                                    jnp.max(s_ctx, axis=-1, keepdims=True)), s_sink)
        p_loc = jnp.exp(s_loc - m)
        p_ctx = jnp.exp(s_ctx - m)
        denom = (jnp.sum(p_loc, axis=-1, keepdims=True) + jnp.sum(p_ctx, axis=-1, keepdims=True)
                 + jnp.exp(s_sink - m))
        o = (jnp.dot(p_loc.astype(BF16), vl, preferred_element_type=F32)
             + jnp.dot(p_ctx.astype(BF16), vx_ref[:, hs], preferred_element_type=F32))
        o = o * (1.0 / denom)
        for g in range(group):
            hq = h * group + g
            o_ref[:, hq * HEAD_DIM:(hq + 1) * HEAD_DIM] = o[g * tq:(g + 1) * tq].astype(BF16)


def _attention(proj, sink, *, n_batch, n_seq, n_ctx, q_col, k_col, v_col, q_width, kv_width):
    r = proj.shape[0]
    t = n_batch * n_seq
    nb_seq = n_seq // TQ
    n_lat_tiles = t // TQ
    n_kv = kv_width // HEAD_DIM
    group = q_width // kv_width
    assert n_ctx % TQ == 0 and q_col % q_width == 0 and k_col % kv_width == 0 and v_col % kv_width == 0
    ctx_tiles = n_ctx // TQ
    kcb, vcb, qcb = k_col // kv_width, v_col // kv_width, q_col // q_width

    def lat(i):
        return i < n_lat_tiles

    def prev_idx(i):
        return jnp.where(jnp.logical_and(lat(i), i % nb_seq > 0), i - 1, i)

    def next_idx(i):
        return jnp.where(jnp.logical_and(lat(i), i % nb_seq < nb_seq - 1), i + 1, i)

    def ctx_idx(i):
        b = jnp.where(lat(i), i // nb_seq, (i - n_lat_tiles) // ctx_tiles)
        return t // n_ctx + b

    assert t % n_ctx == 0
    kern = functools.partial(_attn_kernel, nb_seq=nb_seq, n_lat_tiles=n_lat_tiles, n_kv=n_kv, group=group)
    kv_spec = lambda f, cb: pl.BlockSpec((TQ, kv_width), lambda i: (f(i), cb))
    return pl.pallas_call(
        kern,
        grid=(r // TQ,),
        in_specs=[pl.BlockSpec(memory_space=pltpu.SMEM),
                  pl.BlockSpec((TQ, q_width), lambda i: (i, qcb)),
                  kv_spec(prev_idx, kcb), kv_spec(lambda i: i, kcb), kv_spec(next_idx, kcb),
                  kv_spec(prev_idx, vcb), kv_spec(lambda i: i, vcb), kv_spec(next_idx, vcb),
                  pl.BlockSpec((n_ctx, kv_width), lambda i: (ctx_idx(i), kcb)),
                  pl.BlockSpec((n_ctx, kv_width), lambda i: (ctx_idx(i), vcb))],
        out_specs=pl.BlockSpec((TQ, q_width), lambda i: (i, 0)),
        out_shape=jax.ShapeDtypeStruct((r, q_width), BF16),
        compiler_params=_cparams("parallel"),
        name="window_attention",
    )(sink, proj, proj, proj, proj, proj, proj, proj, proj, proj)


def _merge_kernel(x_ref, attn_ref, cb_ref, cc_ref, cu_ref, ga_ref, gc_ref, ccp_ref, cup_ref, ccn_ref, cun_ref,
                  mod_ref, cw_ref, wa_ref, wc_ref, wo_ref, g_ref, b_ref, o_ref, *, tiles_per_seq, n_lat_tiles, alpha):
    i = pl.program_id(0)
    tm = x_ref.shape[0]
    hr = ccp_ref.shape[0]
    p = i % tiles_per_seq
    is_lat = i < n_lat_tiles
    first = jnp.where(is_lat, p == 0, True)
    last = jnp.where(is_lat, p == tiles_per_seq - 1, True)

    up = cc_ref[...].astype(F32) * cu_ref[...].astype(F32)
    prev_row = ccp_ref[hr - 1:hr, :].astype(F32) * cup_ref[hr - 1:hr, :].astype(F32)
    next_row = ccn_ref[0:1, :].astype(F32) * cun_ref[0:1, :].astype(F32)
    prev_row = jnp.where(first, 0.0, prev_row)
    next_row = jnp.where(last, 0.0, next_row)
    ridx = lax.broadcasted_iota(jnp.int32, up.shape, 0)
    dn = jnp.where(ridx == 0, prev_row, pltpu.roll(up, 1, 0))
    un = jnp.where(ridx == tm - 1, next_row, pltpu.roll(up, tm - 1, 0))
    conv = cw_ref[0:1, :] * dn + cw_ref[1:2, :] * up + cw_ref[2:3, :] * un
    sconv = (cb_ref[...].astype(F32) * conv).astype(BF16)

    a = jnp.dot(attn_ref[...], wa_ref[...], preferred_element_type=F32)
    s = jnp.dot(sconv, wc_ref[...], preferred_element_type=F32)
    m = (ga_ref[...].astype(F32) * a + gc_ref[...].astype(F32) * s).astype(BF16)
    y = jnp.dot(m, wo_ref[...], preferred_element_type=F32)
    z = alpha * x_ref[...] + mod_ref[0, 2:3, :] * y
    o_ref[...] = _layer_norm(z, g_ref[...], b_ref[...])


def _merge(xall, attn, proj, mod, conv_w, wa, wc, wo, g, b, *, seg_of_row_tile, d, n_seq, n_ctx, n_batch, alpha):
    r = xall.shape[0]
    hr = 16
    assert n_seq % TM == 0 and n_ctx % TM == 0 and n_ctx == TM
    tiles_per_seq = n_seq // TM
    n_lat_tiles = n_batch * tiles_per_seq
    kern = functools.partial(_merge_kernel, tiles_per_seq=tiles_per_seq, n_lat_tiles=n_lat_tiles, alpha=alpha)
    col = lambda cb: pl.BlockSpec((TM, d), lambda i: (i, cb))
    rpt = TM // hr
    n_hblk = r // hr
    prev = lambda cb: pl.BlockSpec((hr, d), lambda i: (jnp.maximum(i * rpt - 1, 0), cb))
    nxt = lambda cb: pl.BlockSpec((hr, d), lambda i: (jnp.minimum((i + 1) * rpt, n_hblk - 1), cb))
    const = lambda shape: pl.BlockSpec(shape, lambda i: (0,) * len(shape), pipeline_mode=pl.Buffered(1))
    return pl.pallas_call(
        kern,
        grid=(r // TM,),
        in_specs=[pl.BlockSpec((TM, d), lambda i: (i, 0)),
                  pl.BlockSpec((TM, attn.shape[1]), lambda i: (i, 0)),
                  col(0), col(1), col(2), col(3), col(4),
                  prev(1), prev(2), nxt(1), nxt(2),
                  pl.BlockSpec((1, 6, d), lambda i: (seg_of_row_tile(i, TM), 0, 0)),
                  const((3, d)), const(wa.shape), const(wc.shape), const(wo.shape),
                  const((1, d)), const((1, d))],
        out_specs=pl.BlockSpec((TM, d), lambda i: (i, 0)),
        out_shape=jax.ShapeDtypeStruct((r, d), F32),
        compiler_params=_cparams("parallel"),
        name="merge_ln1",
    )(xall, attn, proj, proj, proj, proj, proj, proj, proj, proj, proj, mod, conv_w, wa, wc, wo,
      g.reshape(1, d), b.reshape(1, d))


def _router_kernel(x_ref, mod_ref, rwt_ref, rb_ref, e_ref, w_ref, rank_ref, cnt_ref, cnt_scr, *, n_groups):
    i = pl.program_id(0)
    tm = x_ref.shape[0]
    n_exp = rwt_ref.shape[0]
    per = n_exp // n_groups

    @pl.when(i == 0)
    def _():
        cnt_scr[...] = jnp.zeros_like(cnt_scr)

    h2 = x_ref[...] * (1.0 + mod_ref[0, 4:5, :]) + mod_ref[0, 3:4, :]
    logits = lax.dot_general(rwt_ref[...], h2, NT_DIMS, preferred_element_type=F32, precision=HIGHEST)
    ex = jnp.exp(logits - jnp.max(logits, axis=0, keepdims=True))
    probs = ex / jnp.sum(ex, axis=0, keepdims=True)
    sel3 = (probs + rb_ref[...]).reshape(n_groups, per, tm)

    li = lax.broadcasted_iota(jnp.int32, (n_groups, per, tm), 1).astype(F32)
    m1 = jnp.max(sel3, axis=1, keepdims=True)
    i1 = jnp.min(jnp.where(sel3 == m1, li, float(per)), axis=1, keepdims=True)
    sel3b = jnp.where(li == i1, -jnp.inf, sel3)
    m2 = jnp.max(sel3b, axis=1, keepdims=True)
    i2 = jnp.min(jnp.where(sel3b == m2, li, float(per)), axis=1, keepdims=True)
    score = m1 + m2
    gi = lax.broadcasted_iota(jnp.int32, (n_groups, 1, tm), 0).astype(F32)
    gbest = jnp.min(jnp.where(score == jnp.max(score, axis=0, keepdims=True), gi, float(n_groups)),
                    axis=0, keepdims=True)
    pick = gi == gbest
    l1 = jnp.sum(jnp.where(pick, i1, 0.0), axis=0)
    l2 = jnp.sum(jnp.where(pick, i2, 0.0), axis=0)
    e1 = gbest[0] * per + l1
    e2 = gbest[0] * per + l2

    ei = lax.broadcasted_iota(jnp.int32, (n_exp, tm), 0).astype(F32)
    is1 = ei == e1
    is2 = ei == e2
    p1 = jnp.sum(jnp.where(is1, probs, 0.0), axis=0, keepdims=True)
    p2 = jnp.sum(jnp.where(is2, probs, 0.0), axis=0, keepdims=True)
    psum = p1 + p2

    onehot = jnp.where(jnp.logical_or(is1, is2), 1.0, 0.0)
    srow = lax.broadcasted_iota(jnp.int32, (tm, tm), 0)
    scol = lax.broadcasted_iota(jnp.int32, (tm, tm), 1)
    before = jnp.where(srow < scol, 1.0, 0.0).astype(BF16)
    excl = jnp.dot(onehot.astype(BF16), before, preferred_element_type=F32) + cnt_scr[:, 0:1]
    r1 = jnp.sum(jnp.where(is1, excl, 0.0), axis=0, keepdims=True)
    r2 = jnp.sum(jnp.where(is2, excl, 0.0), axis=0, keepdims=True)
    cnt_scr[...] = cnt_scr[...] + jnp.sum(onehot, axis=1, keepdims=True)

    e_ref[0:1, :] = e1.astype(jnp.int32)
    e_ref[1:2, :] = e2.astype(jnp.int32)
    w_ref[0:1, :] = p1 / psum
    w_ref[1:2, :] = p2 / psum
    rank_ref[0:1, :] = r1.astype(jnp.int32)
    rank_ref[1:2, :] = r2.astype(jnp.int32)
    cnt_ref[...] = cnt_scr[...]


def _router(x1, mod, rwt, rb, *, seg_of_row_tile, d):
    r = x1.shape[0]
    n_exp = rwt.shape[0]
    row2 = lambda dt: jax.ShapeDtypeStruct((TOP_K, r), dt)
    spec2 = pl.BlockSpec((TOP_K, TM), lambda i: (0, i))
    return pl.pallas_call(
        functools.partial(_router_kernel, n_groups=N_EXPERT_GROUPS),
        grid=(r // TM,),
        in_specs=[pl.BlockSpec((TM, d), lambda i: (i, 0)),
                  pl.BlockSpec((1, 6, d), lambda i: (seg_of_row_tile(i, TM), 0, 0)),
                  pl.BlockSpec((n_exp, d), lambda i: (0, 0)),
                  pl.BlockSpec((n_exp, 1), lambda i: (0, 0))],
        out_specs=[spec2, spec2, spec2, pl.BlockSpec((n_exp, LANE), lambda i: (0, 0))],
        out_shape=[row2(jnp.int32), row2(F32), row2(jnp.int32), jax.ShapeDtypeStruct((n_exp, LANE), F32)],
        scratch_shapes=[pltpu.VMEM((n_exp, LANE), F32)],
        compiler_params=_cparams("arbitrary"),
        name="router",
    )(x1, mod, rwt, rb.reshape(n_exp, 1))


def _dispatch_kernel(dest_ref, fill_lo_ref, fill_hi_ref, x_ref, mod_ref, xs_ref, buf, zrow, sems, zsem, *, n_rows):
    i = pl.program_id(0)
    n_steps = pl.num_programs(0)
    tm = x_ref.shape[0]
    slot = i % 2
    buf[slot] = x_ref[...] * (1.0 + mod_ref[0, 4:5, :]) + mod_ref[0, 3:4, :]

    def row_copy(step, sl, t, k):
        dst = dest_ref[k * n_rows + step * tm + t]
        return pltpu.make_async_copy(buf.at[sl, pl.ds(t, 1)], xs_ref.at[pl.ds(dst, 1)], sems.at[sl])

    def issue(t, carry):
        for k in range(TOP_K):
            row_copy(i, slot, t, k).start()
        return carry

    lax.fori_loop(0, tm, issue, 0)

    def drain(step, sl):
        def body(t, carry):
            for k in range(TOP_K):
                row_copy(step, sl, t, k).wait()
            return carry
        lax.fori_loop(0, tm, body, 0)

    @pl.when(i > 0)
    def _():
        drain(i - 1, 1 - slot)

    @pl.when(i == n_steps - 1)
    def _():
        drain(i, slot)
        zrow[...] = jnp.zeros_like(zrow)
        n_ranges = fill_lo_ref.shape[0]

        def zero_copy(rr):
            return pltpu.make_async_copy(zrow.at[pl.ds(0, 1)], xs_ref.at[pl.ds(rr, 1)], zsem)

        def per_expert(fn):
            def body(e, carry):
                lax.fori_loop(fill_lo_ref[e], fill_hi_ref[e], lambda rr, c: (fn(rr), c)[1], 0)
                return carry
            lax.fori_loop(0, n_ranges, body, 0)

        per_expert(lambda rr: zero_copy(rr).start())
        per_expert(lambda rr: zero_copy(rr).wait())


def _dispatch(x1, mod, dest_flat, fill_lo, fill_hi, cap, *, seg_of_row_tile, d):
    r = x1.shape[0]
    grid_spec = pltpu.PrefetchScalarGridSpec(
        num_scalar_prefetch=3,
        grid=(r // TM,),
        in_specs=[pl.BlockSpec((TM, d), lambda i, *_: (i, 0)),
                  pl.BlockSpec((1, 6, d), lambda i, *_: (seg_of_row_tile(i, TM), 0, 0))],
        out_specs=pl.BlockSpec(memory_space=pl.ANY),
        scratch_shapes=[pltpu.VMEM((2, TM, d), F32), pltpu.VMEM((8, d), F32),
                        pltpu.SemaphoreType.DMA((2,)), pltpu.SemaphoreType.DMA(())],
    )
    return pl.pallas_call(
        functools.partial(_dispatch_kernel, n_rows=r),
        grid_spec=grid_spec,
        out_shape=jax.ShapeDtypeStruct((cap, d), F32),
        compiler_params=_cparams("arbitrary"),
        name="moe_dispatch",
    )(dest_flat, fill_lo, fill_hi, x1, mod)


def _expert_kernel(blk_e_ref, n_used_ref, xs_ref, wg_ref, wu_ref, wd_ref, o_ref):
    b = pl.program_id(0)

    @pl.when(b < n_used_ref[0])
    def _():
        xb = xs_ref[...].astype(BF16)
        g = jnp.dot(xb, wg_ref[...], preferred_element_type=F32)
        u = jnp.dot(xb, wu_ref[...], preferred_element_type=F32)
        hmid = (g * jax.nn.sigmoid(g) * u).astype(BF16)
        o_ref[...] = jnp.dot(hmid, wd_ref[...], preferred_element_type=F32)

    @pl.when(b >= n_used_ref[0])
    def _():
        o_ref[...] = jnp.zeros_like(o_ref)


def _experts(xs, blk_expert, n_used, wg, wu, wd):
    cap, d = xs.shape
    ff = wg.shape[2]
    xrow = lambda b, be, nu: (jnp.minimum(b, nu[0] - 1), 0)
    grid_spec = pltpu.PrefetchScalarGridSpec(
        num_scalar_prefetch=2,
        grid=(cap // MOE_BLOCK,),
        in_specs=[pl.BlockSpec((MOE_BLOCK, d), xrow),
                  pl.BlockSpec((None, d, ff), lambda b, be, nu: (be[b], 0, 0)),
                  pl.BlockSpec((None, d, ff), lambda b, be, nu: (be[b], 0, 0)),
                  pl.BlockSpec((None, ff, d), lambda b, be, nu: (be[b], 0, 0))],
        out_specs=pl.BlockSpec((MOE_BLOCK, d), lambda b, be, nu: (b, 0)),
    )
    return pl.pallas_call(
        _expert_kernel,
        grid_spec=grid_spec,
        out_shape=jax.ShapeDtypeStruct((cap, d), F32),
        compiler_params=_cparams("arbitrary"),
        name="moe_experts",
    )(blk_expert, n_used, xs, wg, wu, wd)


def _combine_kernel(dest_ref, x_ref, wt_ref, mod_ref, g_ref, b_ref, ys_ref, o_ref, buf, sem, *, n_rows, alpha):
    i = pl.program_id(0)
    tm = x_ref.shape[0]

    def row_copy(t, k):
        src = dest_ref[k * n_rows + i * tm + t]
        return pltpu.make_async_copy(ys_ref.at[pl.ds(src, 1)], buf.at[k, pl.ds(t, 1)], sem)

    def issue(t, carry):
        for k in range(TOP_K):
            row_copy(t, k).start()
        return carry

    def drain(t, carry):
        for k in range(TOP_K):
            row_copy(t, k).wait()
        return carry

    lax.fori_loop(0, tm, issue, 0)
    lax.fori_loop(0, tm, drain, 0)
    f = wt_ref[:, 0:1] * buf[0] + wt_ref[:, 1:2] * buf[1]
    z = alpha * x_ref[...] + mod_ref[0, 5:6, :] * f
    o_ref[...] = _layer_norm(z, g_ref[...], b_ref[...])


def _combine(x1, wt_rows, mod, g, b, dest_flat, ys, *, seg_of_row_tile, d, alpha):
    r = x1.shape[0]
    grid_spec = pltpu.PrefetchScalarGridSpec(
        num_scalar_prefetch=1,
        grid=(r // TM,),
        in_specs=[pl.BlockSpec((TM, d), lambda i, *_: (i, 0)),
                  pl.BlockSpec((TM, TOP_K), lambda i, *_: (i, 0)),
                  pl.BlockSpec((1, 6, d), lambda i, *_: (seg_of_row_tile(i, TM), 0, 0)),
                  pl.BlockSpec((1, d), lambda i, *_: (0, 0)),
                  pl.BlockSpec((1, d), lambda i, *_: (0, 0)),
                  pl.BlockSpec(memory_space=pl.ANY)],
        out_specs=pl.BlockSpec((TM, d), lambda i, *_: (i, 0)),
        scratch_shapes=[pltpu.VMEM((TOP_K, TM, d), F32), pltpu.SemaphoreType.DMA(())],
    )
    return pl.pallas_call(
        functools.partial(_combine_kernel, n_rows=r, alpha=alpha),
        grid_spec=grid_spec,
        out_shape=jax.ShapeDtypeStruct((r, d), F32),
        compiler_params=_cparams("arbitrary"),
        name="moe_combine_ln2",
    )(dest_flat, x1, wt_rows, mod, g.reshape(1, d), b.reshape(1, d), ys)


def _rope_tables(n_batch, n_seq, n_ctx_rows):
    nf = HEAD_DIM // 4
    t = jnp.arange(n_seq, dtype=jnp.int32)
    row = (t // GRID_W).astype(F32)
    colp = (t % GRID_W).astype(F32)
    inv_freq = jnp.power(ROPE_BASE, -jnp.arange(nf, dtype=F32) / nf)
    ar, ac = row[:, None] * inv_freq, colp[:, None] * inv_freq
    cos_t = jnp.concatenate([jnp.cos(ar), jnp.cos(ar), jnp.cos(ac), jnp.cos(ac)], axis=1)
    sin_t = jnp.concatenate([-jnp.sin(ar), jnp.sin(ar), -jnp.sin(ac), jnp.sin(ac)], axis=1)
    cos_all = jnp.concatenate([jnp.tile(cos_t, (n_batch, 1)), jnp.ones((n_ctx_rows, HEAD_DIM), F32)], axis=0)
    sin_all = jnp.concatenate([jnp.tile(sin_t, (n_batch, 1)), jnp.zeros((n_ctx_rows, HEAD_DIM), F32)], axis=0)
    return cos_all, sin_all


def kernel(x, c, ctx, c_ctx, w_ada, b_ada, w_in, attn_sink, conv_w, w_attn_proj, w_conv_proj, w_out,
           ln1_g, ln1_b, ln2_g, ln2_b, router_w, router_b, w_gate, w_up, w_down):
    n_batch, n_seq, d = x.shape
    n_ctx = ctx.shape[1]
    depth = w_ada.shape[0]
    n_exp = router_w.shape[1]
    q_width = attn_sink.shape[1] * HEAD_DIM
    in_width = w_in.shape[2]
    kv_width = (in_width - q_width - 5 * d) // 2
    base = q_width + 2 * kv_width
    t = n_batch * n_seq
    r = t + n_batch * n_ctx
    alpha = float((2 * depth) ** 0.25)
    assert w_conv_proj.shape[1] == d and n_seq % GRID_W == 0
    assert r % TM_IN == 0 and r % TM == 0 and r % TQ == 0 and n_seq % TM_IN == 0

    def seg_of_row_tile(i, tile):
        return jnp.minimum((i * tile) // n_seq, n_batch)

    xall = jnp.concatenate([x.reshape(t, d), ctx.reshape(n_batch * n_ctx, d)], axis=0)
    rope_c, rope_s = _rope_tables(n_batch, n_seq, n_batch * n_ctx)
    c_rows = jnp.concatenate([c, c_ctx[None, :], jnp.zeros((8 - n_batch - 1, d), F32)], axis=0)
    rwt = router_w.T
    n_assign = r * TOP_K
    cap = (n_assign + n_exp * (MOE_BLOCK - 1) + MOE_BLOCK - 1) // MOE_BLOCK * MOE_BLOCK
    n_blk = cap // MOE_BLOCK
    q_col, k_col, v_col = 5 * d, 5 * d + q_width, 5 * d + q_width + kv_width

    for l in range(depth):
        mod = _mod_table(c_rows, w_ada[l], b_ada[l]).reshape(8, 6, d)
        w_perm = jnp.concatenate([w_in[l][:, base:], w_in[l][:, :base]], axis=1).astype(BF16)
        proj = _in_proj(xall, mod, w_perm, rope_c, rope_s, seg_of_row_tile=seg_of_row_tile, d=d,
                        q_width=q_width, kv_width=kv_width)
        attn = _attention(proj, attn_sink[l], n_batch=n_batch, n_seq=n_seq, n_ctx=n_ctx, q_col=q_col, k_col=k_col,
                          v_col=v_col, q_width=q_width, kv_width=kv_width)
        x1 = _merge(xall, attn, proj, mod, conv_w[l], w_attn_proj[l].astype(BF16), w_conv_proj[l].astype(BF16),
                    w_out[l].astype(BF16), ln1_g[l], ln1_b[l], seg_of_row_tile=seg_of_row_tile, d=d, n_seq=n_seq,
                    n_ctx=n_ctx, n_batch=n_batch, alpha=alpha)

        e_idx, wts, rank, cnt = _router(x1, mod, rwt, router_b, seg_of_row_tile=seg_of_row_tile, d=d)
        counts = cnt[:, 0].astype(jnp.int32)
        padded = (counts + MOE_BLOCK - 1) // MOE_BLOCK * MOE_BLOCK
        pend = jnp.cumsum(padded)
        pstart = pend - padded
        dest_flat = (jnp.take(pstart, e_idx) + rank).reshape(-1)
        n_used = (pend[-1] // MOE_BLOCK).astype(jnp.int32)
        blk_start = jnp.minimum(jnp.arange(n_blk, dtype=jnp.int32), n_used - 1) * MOE_BLOCK
        blk_expert = jnp.minimum(jnp.searchsorted(pend, blk_start, side='right'), n_exp - 1).astype(jnp.int32)
        fill_lo = jnp.concatenate([pstart + counts, pend[-1:]]).astype(jnp.int32)
        fill_hi = jnp.concatenate([pend, jnp.full((1,), cap, pend.dtype)]).astype(jnp.int32)
        xs = _dispatch(x1, mod, dest_flat, fill_lo, fill_hi, cap, seg_of_row_tile=seg_of_row_tile, d=d)
        ys = _experts(xs, blk_expert, n_used.reshape(1), w_gate[l].astype(BF16), w_up[l].astype(BF16),
                      w_down[l].astype(BF16))
        xall = _combine(x1, wts.T, mod, ln2_g[l], ln2_b[l], dest_flat, ys, seg_of_row_tile=seg_of_row_tile, d=d,
                        alpha=alpha)

    return xall[:t].reshape(n_batch, n_seq, d)
```

```python
import functools

import numpy as np
import jax
import jax.numpy as jnp
from jax import lax
from jax.experimental import pallas as pl
from jax.experimental.pallas import tpu as pltpu

F32 = jnp.float32
BF16 = jnp.bfloat16
U32 = jnp.uint32
HIGHEST = lax.Precision.HIGHEST

GRID_W = 64
HEAD_DIM = 128
WINDOW = 128
ROPE_BASE = 10000.0
N_EXPERT_GROUPS = 4
TOP_K = 2
LN_EPS = 1e-5
NEG_INF = -1e30

LANE = 128
SUBLANE = 8
TQ = WINDOW
TM = 256
TM_IN = 512
TN_IN = 1024
TN_MOD = 1024
MOE_BLOCK = 256
ISSUE_UNROLL = 8
VMEM_LIMIT = 56 * 1024 * 1024

NT_DIMS = (((1,), (1,)), ((), ()))


def _cparams(*sem):
    return pltpu.CompilerParams(dimension_semantics=sem, vmem_limit_bytes=VMEM_LIMIT)


def _layer_norm(z, g, b):
    mu = jnp.mean(z, axis=-1, keepdims=True)
    zc = z - mu
    var = jnp.mean(zc * zc, axis=-1, keepdims=True)
    return zc * lax.rsqrt(var + LN_EPS) * g + b


def _pack_rows(v):
    half = v.shape[1] // 2
    bits = lax.bitcast_convert_type(v.astype(BF16).astype(F32), U32)
    words = []
    for s in range(half // LANE):
        lo = bits[:, s * LANE:(s + 1) * LANE] >> 16
        hi = bits[:, half + s * LANE:half + (s + 1) * LANE] & np.uint32(0xFFFF0000)
        words.append(lo | hi)
    return words


def _unpack_rows(words):
    lo = [lax.bitcast_convert_type(w << 16, F32) for w in words]
    hi = [lax.bitcast_convert_type(w & np.uint32(0xFFFF0000), F32) for w in words]
    return jnp.concatenate(lo + hi, axis=1)


def _slab_rows(d):
    assert d % (2 * LANE) == 0 and d // (2 * LANE) == SUBLANE, "one row must pack into one (8, 128) uint32 tile"
    return d // (2 * LANE)


def _mod_kernel(c_ref, w_ref, b_ref, o_ref):
    cc = c_ref[...]
    s = cc * jax.nn.sigmoid(cc)
    o_ref[...] = jnp.dot(s, w_ref[...], preferred_element_type=F32, precision=HIGHEST) + b_ref[...]


def _mod_table(c_rows, w_ada, b_ada_l, layer):
    rows, d = c_rows.shape
    n_out = w_ada.shape[2]
    return pl.pallas_call(
        _mod_kernel,
        grid=(n_out // TN_MOD,),
        in_specs=[pl.BlockSpec((rows, d), lambda j: (0, 0)),
                  pl.BlockSpec((None, d, TN_MOD), lambda j: (layer, 0, j)),
                  pl.BlockSpec((1, TN_MOD), lambda j: (0, j))],
        out_specs=pl.BlockSpec((rows, TN_MOD), lambda j: (0, j)),
        out_shape=jax.ShapeDtypeStruct((rows, n_out), F32),
        compiler_params=_cparams("parallel"),
        name="mod_table",
    )(c_rows, w_ada, b_ada_l.reshape(1, n_out))


def _cast_kernel(w_ref, o_ref):
    o_ref[...] = w_ref[...].astype(BF16)


def _cast_bf16(w, layer):
    _, e, a, b = w.shape
    return pl.pallas_call(
        _cast_kernel,
        grid=(e,),
        in_specs=[pl.BlockSpec((None, 1, a, b), lambda i: (layer, i, 0, 0))],
        out_specs=pl.BlockSpec((1, a, b), lambda i: (i, 0, 0)),
        out_shape=jax.ShapeDtypeStruct((e, a, b), BF16),
        compiler_params=_cparams("parallel"),
        name="cast_bf16",
    )(w)


def _modulate_kernel(xa_ref, xb_ref, mod_ref, o_ref, *, n_lat_tiles):
    x = jnp.where(pl.program_id(0) < n_lat_tiles, xa_ref[...], xb_ref[...])
    o_ref[...] = (x * (1.0 + mod_ref[0, 1:2, :]) + mod_ref[0, 0:1, :]).astype(BF16)


def _modulate(x_lat, x_ctx, mod, *, seg_of_row_tile):
    t, d = x_lat.shape
    r = t + x_ctx.shape[0]
    n_lat = t // TM
    return pl.pallas_call(
        functools.partial(_modulate_kernel, n_lat_tiles=n_lat),
        grid=(r // TM,),
        in_specs=[pl.BlockSpec((TM, d), lambda i: (jnp.minimum(i, n_lat - 1), 0)),
                  pl.BlockSpec((TM, d), lambda i: (jnp.maximum(i - n_lat, 0), 0)),
                  pl.BlockSpec((1, 6, d), lambda i: (seg_of_row_tile(i, TM), 0, 0))],
        out_specs=pl.BlockSpec((TM, d), lambda i: (i, 0)),
        out_shape=jax.ShapeDtypeStruct((r, d), BF16),
        compiler_params=_cparams("parallel"),
        name="modulate_in",
    )(x_lat, x_ctx, mod)


def _inproj_kernel(h_ref, w_ref, rc_ref, rs_ref, o_ref, wb_scr, *, n_plain, n_gate, n_q, k_cols, scale):
    j = pl.program_id(0)
    tn = o_ref.shape[1]

    @pl.when(pl.program_id(1) == 0)
    def _():
        wb_scr[...] = w_ref[...].astype(BF16)

    acc = jnp.dot(h_ref[...], wb_scr[...], preferred_element_type=F32)

    def rope(xh):
        lane = lax.broadcasted_iota(jnp.int32, xh.shape, 1)
        sw = jnp.where((lane & 32) == 0, pltpu.roll(xh, LANE - 32, 1), pltpu.roll(xh, 32, 1))
        return xh * rc_ref[...] + sw * rs_ref[...]

    @pl.when(j < n_plain)
    def _():
        o_ref[...] = acc.astype(BF16)

    @pl.when((j >= n_plain) & (j < n_plain + n_gate))
    def _():
        o_ref[...] = jax.nn.sigmoid(acc).astype(BF16)

    @pl.when((j >= n_plain + n_gate) & (j < n_plain + n_gate + n_q))
    def _():
        for c in range(tn // LANE):
            sl = slice(c * LANE, (c + 1) * LANE)
            o_ref[:, sl] = (rope(acc[:, sl]) * scale).astype(BF16)

    @pl.when(j == n_plain + n_gate + n_q)
    def _():
        for c in range(tn // LANE):
            sl = slice(c * LANE, (c + 1) * LANE)
            if c * LANE < k_cols:
                o_ref[:, sl] = rope(acc[:, sl]).astype(BF16)
            else:
                o_ref[:, sl] = acc[:, sl].astype(BF16)


def _in_proj(h, w_in, layer, rope_c, rope_s, *, d, q_width, kv_width, n_lat_rows, n_seq):
    r = h.shape[0]
    in_width = w_in.shape[2]
    base = q_width + 2 * kv_width
    assert (3 * d) % TN_IN == 0 and (2 * d) % TN_IN == 0 and q_width % TN_IN == 0 and 2 * kv_width == TN_IN
    n_plain, n_gate, n_q = 3 * d // TN_IN, 2 * d // TN_IN, q_width // TN_IN
    n_j = in_width // TN_IN
    assert n_plain + n_gate + n_q + 1 == n_j and base % TN_IN == 0 and n_seq % TM_IN == 0
    shift = base // TN_IN
    tiles_per_seq = n_seq // TM_IN
    n_lat_tiles = n_lat_rows // TM_IN

    def pos_tile(i):
        return jnp.where(i < n_lat_tiles, i % tiles_per_seq, tiles_per_seq)

    kern = functools.partial(_inproj_kernel, n_plain=n_plain, n_gate=n_gate, n_q=n_q, k_cols=kv_width,
                             scale=HEAD_DIM ** -0.5)
    return pl.pallas_call(
        kern,
        grid=(n_j, r // TM_IN),
        in_specs=[pl.BlockSpec((TM_IN, d), lambda j, i: (i, 0)),
                  pl.BlockSpec((None, d, TN_IN), lambda j, i: (layer, 0, (j + shift) % n_j)),
                  pl.BlockSpec((TM_IN, LANE), lambda j, i: (pos_tile(i), 0)),
                  pl.BlockSpec((TM_IN, LANE), lambda j, i: (pos_tile(i), 0))],
        out_specs=pl.BlockSpec((TM_IN, TN_IN), lambda j, i: (i, j)),
        out_shape=jax.ShapeDtypeStruct((r, in_width), BF16),
        scratch_shapes=[pltpu.VMEM((d, TN_IN), BF16)],
        compiler_params=_cparams("arbitrary", "arbitrary"),
        name="in_proj",
    )(h, w_in, rope_c, rope_s)


def _attn_kernel(sink_ref, q_ref, kp_ref, kc_ref, kn_ref, vp_ref, vc_ref, vn_ref, kx_ref, vx_ref, o_ref, *,
                 nb_seq, n_lat_tiles, n_kv, group):
    i = pl.program_id(0)
    tq = q_ref.shape[0]
    is_lat = i < n_lat_tiles
    p = i % nb_seq
    has_prev = jnp.logical_and(is_lat, p > 0)
    has_next = jnp.logical_and(is_lat, p < nb_seq - 1)
    row = lax.broadcasted_iota(jnp.int32, (tq, tq), 0)
    col = lax.broadcasted_iota(jnp.int32, (tq, tq), 1)
    m_prev = jnp.logical_and(col >= row, has_prev)
    m_cur = jnp.logical_and(col >= 0, is_lat)
    m_next = jnp.logical_and(col <= row, has_next)
    mask1 = jnp.concatenate([m_prev, m_cur, m_next], axis=1)
    mask = jnp.concatenate([mask1] * group, axis=0)

    for h in range(n_kv):
        hs = slice(h * HEAD_DIM, (h + 1) * HEAD_DIM)
        q4 = jnp.concatenate([q_ref[:, (h * group + g) * HEAD_DIM:(h * group + g + 1) * HEAD_DIM]
                              for g in range(group)], axis=0)
        kl = jnp.concatenate([kp_ref[:, hs], kc_ref[:, hs], kn_ref[:, hs]], axis=0)
        vl = jnp.concatenate([vp_ref[:, hs], vc_ref[:, hs], vn_ref[:, hs]], axis=0)
        s_loc = lax.dot_general(q4, kl, NT_DIMS, preferred_element_type=F32)
        s_loc = jnp.where(mask, s_loc, NEG_INF)
        s_ctx = lax.dot_general(q4, kx_ref[:, hs], NT_DIMS, preferred_element_type=F32)
        s_sink = jnp.concatenate([jnp.full((tq, 1), sink_ref[h * group + g], F32) for g in range(group)], axis=0)
        m = jnp.maximum(jnp.maximum(jnp.max(s_loc, axis=-1, keepdims=True),
                                    jnp.max(s_ctx, axis=-1, keepdims=True)), s_sink)
        p_loc = jnp.exp(s_loc - m)
        p_ctx = jnp.exp(s_ctx - m)
        denom = (jnp.sum(p_loc, axis=-1, keepdims=True) + jnp.sum(p_ctx, axis=-1, keepdims=True)
                 + jnp.exp(s_sink - m))
        o = (jnp.dot(p_loc.astype(BF16), vl, preferred_element_type=F32)
             + jnp.dot(p_ctx.astype(BF16), vx_ref[:, hs], preferred_element_type=F32))
        o = o * (1.0 / denom)
        for g in range(group):
            hq = h * group + g
            o_ref[:, hq * HEAD_DIM:(hq + 1) * HEAD_DIM] = o[g * tq:(g + 1) * tq].astype(BF16)


def _attention(proj, sink, *, n_rows, n_batch, n_seq, n_ctx, q_col, k_col, v_col, q_width, kv_width):
    t = n_batch * n_seq
    nb_seq = n_seq // TQ
    n_lat_tiles = t // TQ
    n_kv = kv_width // HEAD_DIM
    group = q_width // kv_width
    assert n_ctx % TQ == 0 and q_col % q_width == 0 and k_col % kv_width == 0 and v_col % kv_width == 0
    ctx_tiles = n_ctx // TQ
    kcb, vcb, qcb = k_col // kv_width, v_col // kv_width, q_col // q_width

    def lat(i):
        return i < n_lat_tiles

    def prev_idx(i):
        return jnp.where(jnp.logical_and(lat(i), i % nb_seq > 0), i - 1, i)

    def next_idx(i):
        return jnp.where(jnp.logical_and(lat(i), i % nb_seq < nb_seq - 1), i + 1, i)

    def ctx_idx(i):
        b = jnp.where(lat(i), i // nb_seq, (i - n_lat_tiles) // ctx_tiles)
        return t // n_ctx + b

    assert t % n_ctx == 0
    kern = functools.partial(_attn_kernel, nb_seq=nb_seq, n_lat_tiles=n_lat_tiles, n_kv=n_kv, group=group)
    kv_spec = lambda f, cb: pl.BlockSpec((TQ, kv_width), lambda i: (f(i), cb))
    return pl.pallas_call(
        kern,
        grid=(n_rows // TQ,),
        in_specs=[pl.BlockSpec(memory_space=pltpu.SMEM),
                  pl.BlockSpec((TQ, q_width), lambda i: (i, qcb)),
                  kv_spec(prev_idx, kcb), kv_spec(lambda i: i, kcb), kv_spec(next_idx, kcb),
                  kv_spec(prev_idx, vcb), kv_spec(lambda i: i, vcb), kv_spec(next_idx, vcb),
                  pl.BlockSpec((n_ctx, kv_width), lambda i: (ctx_idx(i), kcb)),
                  pl.BlockSpec((n_ctx, kv_width), lambda i: (ctx_idx(i), vcb))],
        out_specs=pl.BlockSpec((TQ, q_width), lambda i: (i, 0)),
        out_shape=jax.ShapeDtypeStruct((n_rows, q_width), BF16),
        compiler_params=_cparams("parallel"),
        name="window_attention",
    )(sink, proj, proj, proj, proj, proj, proj, proj, proj, proj)


def _merge_kernel(*refs, two_src, tiles_per_seq, n_lat_tiles, alpha):
    if two_src:
        xa_ref, xb_ref = refs[:2]
        refs = refs[2:]
    else:
        xa_ref = refs[0]
        refs = refs[1:]
    (attn_ref, cb_ref, cc_ref, cu_ref, ga_ref, gc_ref, ccp_ref, cup_ref, ccn_ref, cun_ref,
     mod_ref, cw_ref, wa_ref, wc_ref, wo_ref, g_ref, b_ref, o_ref) = refs
    i = pl.program_id(0)
    tm = xa_ref.shape[0]
    hr = ccp_ref.shape[0]
    p = i % tiles_per_seq
    is_lat = i < n_lat_tiles
    first = jnp.where(is_lat, p == 0, True)
    last = jnp.where(is_lat, p == tiles_per_seq - 1, True)
    x = jnp.where(is_lat, xa_ref[...], xb_ref[...]) if two_src else xa_ref[...]

    up = cc_ref[...].astype(F32) * cu_ref[...].astype(F32)
    prev_row = ccp_ref[hr - 1:hr, :].astype(F32) * cup_ref[hr - 1:hr, :].astype(F32)
    next_row = ccn_ref[0:1, :].astype(F32) * cun_ref[0:1, :].astype(F32)
    prev_row = jnp.where(first, 0.0, prev_row)
    next_row = jnp.where(last, 0.0, next_row)
    ridx = lax.broadcasted_iota(jnp.int32, up.shape, 0)
    dn = jnp.where(ridx == 0, prev_row, pltpu.roll(up, 1, 0))
    un = jnp.where(ridx == tm - 1, next_row, pltpu.roll(up, tm - 1, 0))
    conv = cw_ref[0:1, :] * dn + cw_ref[1:2, :] * up + cw_ref[2:3, :] * un
    sconv = (cb_ref[...].astype(F32) * conv).astype(BF16)

    a = jnp.dot(attn_ref[...], wa_ref[...], preferred_element_type=F32)
    s = jnp.dot(sconv, wc_ref[...], preferred_element_type=F32)
    m = (ga_ref[...].astype(F32) * a + gc_ref[...].astype(F32) * s).astype(BF16)
    y = jnp.dot(m, wo_ref[...], preferred_element_type=F32)
    z = alpha * x + mod_ref[0, 2:3, :] * y
    o_ref[...] = _layer_norm(z, g_ref[...], b_ref[...])


def _merge(x_lat, x_ctx, attn, proj, mod, conv_w, wa, wc, wo, g, b, *, n_rows, seg_of_row_tile, d, n_seq, n_ctx,
           n_batch, alpha):
    hr = 16
    assert n_seq % TM == 0 and n_ctx == TM
    tiles_per_seq = n_seq // TM
    n_lat_tiles = n_batch * tiles_per_seq
    two_src = x_ctx is not None
    kern = functools.partial(_merge_kernel, two_src=two_src, tiles_per_seq=tiles_per_seq, n_lat_tiles=n_lat_tiles,
                             alpha=alpha)
    col = lambda cb: pl.BlockSpec((TM, d), lambda i: (i, cb))
    rpt = TM // hr
    n_hblk = proj.shape[0] // hr
    prev = lambda cb: pl.BlockSpec((hr, d), lambda i: (jnp.maximum(i * rpt - 1, 0), cb))
    nxt = lambda cb: pl.BlockSpec((hr, d), lambda i: (jnp.minimum((i + 1) * rpt, n_hblk - 1), cb))
    const = lambda shape: pl.BlockSpec(shape, lambda i: (0,) * len(shape), pipeline_mode=pl.Buffered(1))
    if two_src:
        x_specs = [pl.BlockSpec((TM, d), lambda i: (jnp.minimum(i, n_lat_tiles - 1), 0)),
                   pl.BlockSpec((TM, d), lambda i: (jnp.maximum(i - n_lat_tiles, 0), 0))]
        x_args = (x_lat, x_ctx)
    else:
        x_specs = [pl.BlockSpec((TM, d), lambda i: (i, 0))]
        x_args = (x_lat,)
    return pl.pallas_call(
        kern,
        grid=(n_rows // TM,),
        in_specs=x_specs + [pl.BlockSpec((TM, attn.shape[1]), lambda i: (i, 0)),
                            col(0), col(1), col(2), col(3), col(4),
                            prev(1), prev(2), nxt(1), nxt(2),
                            pl.BlockSpec((1, 6, d), lambda i: (seg_of_row_tile(i, TM), 0, 0)),
                            const((3, d)), const(wa.shape), const(wc.shape), const(wo.shape),
                            const((1, d)), const((1, d))],
        out_specs=pl.BlockSpec((TM, d), lambda i: (i, 0)),
        out_shape=jax.ShapeDtypeStruct((n_rows, d), F32),
        compiler_params=_cparams("parallel"),
        name="merge_ln1",
    )(*x_args, attn, proj, proj, proj, proj, proj, proj, proj, proj, proj, mod, conv_w, wa, wc, wo,
      g.reshape(1, d), b.reshape(1, d))


def _router_kernel(x_ref, mod_ref, rwt_ref, rb_ref, e_ref, w_ref, rank_ref, cnt_ref, cnt_scr, *, n_groups):
    i = pl.program_id(0)
    tm = x_ref.shape[0]
    n_exp = rwt_ref.shape[0]
    per = n_exp // n_groups

    @pl.when(i == 0)
    def _():
        cnt_scr[...] = jnp.zeros_like(cnt_scr)

    h2 = x_ref[...] * (1.0 + mod_ref[0, 4:5, :]) + mod_ref[0, 3:4, :]
    logits = lax.dot_general(rwt_ref[...], h2, NT_DIMS, preferred_element_type=F32, precision=HIGHEST)
    ex = jnp.exp(logits - jnp.max(logits, axis=0, keepdims=True))
    probs = ex / jnp.sum(ex, axis=0, keepdims=True)
    sel3 = (probs + rb_ref[...]).reshape(n_groups, per, tm)

    li = lax.broadcasted_iota(jnp.int32, (n_groups, per, tm), 1).astype(F32)
    m1 = jnp.max(sel3, axis=1, keepdims=True)
    i1 = jnp.min(jnp.where(sel3 == m1, li, float(per)), axis=1, keepdims=True)
    sel3b = jnp.where(li == i1, -jnp.inf, sel3)
    m2 = jnp.max(sel3b, axis=1, keepdims=True)
    i2 = jnp.min(jnp.where(sel3b == m2, li, float(per)), axis=1, keepdims=True)
    score = m1 + m2
    gi = lax.broadcasted_iota(jnp.int32, (n_groups, 1, tm), 0).astype(F32)
    gbest = jnp.min(jnp.where(score == jnp.max(score, axis=0, keepdims=True), gi, float(n_groups)),
                    axis=0, keepdims=True)
    pick = gi == gbest
    l1 = jnp.sum(jnp.where(pick, i1, 0.0), axis=0)
    l2 = jnp.sum(jnp.where(pick, i2, 0.0), axis=0)
    e1 = gbest[0] * per + l1
    e2 = gbest[0] * per + l2

    ei = lax.broadcasted_iota(jnp.int32, (n_exp, tm), 0).astype(F32)
    is1 = ei == e1
    is2 = ei == e2
    p1 = jnp.sum(jnp.where(is1, probs, 0.0), axis=0, keepdims=True)
    p2 = jnp.sum(jnp.where(is2, probs, 0.0), axis=0, keepdims=True)
    psum = p1 + p2

    onehot = jnp.where(jnp.logical_or(is1, is2), 1.0, 0.0)
    srow = lax.broadcasted_iota(jnp.int32, (tm, tm), 0)
    scol = lax.broadcasted_iota(jnp.int32, (tm, tm), 1)
    before = jnp.where(srow < scol, 1.0, 0.0).astype(BF16)
    excl = jnp.dot(onehot.astype(BF16), before, preferred_element_type=F32) + cnt_scr[:, 0:1]
    r1 = jnp.sum(jnp.where(is1, excl, 0.0), axis=0, keepdims=True)
    r2 = jnp.sum(jnp.where(is2, excl, 0.0), axis=0, keepdims=True)
    cnt_scr[...] = cnt_scr[...] + jnp.sum(onehot, axis=1, keepdims=True)

    e_ref[0:1, :] = e1.astype(jnp.int32)
    e_ref[1:2, :] = e2.astype(jnp.int32)
    w_ref[0:1, :] = p1 / psum
    w_ref[1:2, :] = p2 / psum
    rank_ref[0:1, :] = r1.astype(jnp.int32)
    rank_ref[1:2, :] = r2.astype(jnp.int32)
    cnt_ref[...] = cnt_scr[...]


def _router(x1, mod, rwt, rb, *, seg_of_row_tile, d):
    r = x1.shape[0]
    n_exp = rwt.shape[0]
    row2 = lambda dt: jax.ShapeDtypeStruct((TOP_K, r), dt)
    spec2 = pl.BlockSpec((TOP_K, TM), lambda i: (0, i))
    return pl.pallas_call(
        functools.partial(_router_kernel, n_groups=N_EXPERT_GROUPS),
        grid=(r // TM,),
        in_specs=[pl.BlockSpec((TM, d), lambda i: (i, 0)),
                  pl.BlockSpec((1, 6, d), lambda i: (seg_of_row_tile(i, TM), 0, 0)),
                  pl.BlockSpec((n_exp, d), lambda i: (0, 0)),
                  pl.BlockSpec((n_exp, 1), lambda i: (0, 0))],
        out_specs=[spec2, spec2, spec2, pl.BlockSpec((n_exp, LANE), lambda i: (0, 0))],
        out_shape=[row2(jnp.int32), row2(F32), row2(jnp.int32), jax.ShapeDtypeStruct((n_exp, LANE), F32)],
        scratch_shapes=[pltpu.VMEM((n_exp, LANE), F32)],
        compiler_params=_cparams("arbitrary"),
        name="router",
    )(x1, mod, rwt, rb.reshape(n_exp, 1))


def _dispatch_kernel(dest_ref, fill_lo_ref, fill_hi_ref, x_ref, mod_ref, xs_ref, buf, zslab, sem, zsem, *, n_rows):
    i = pl.program_id(0)
    n_steps = pl.num_programs(0)
    tm = x_ref.shape[0]
    words = _pack_rows(x_ref[...] * (1.0 + mod_ref[0, 4:5, :]) + mod_ref[0, 3:4, :])

    def drain():
        for _ in range(TOP_K):
            pltpu.make_async_copy(buf, xs_ref.at[pl.ds(0, tm * SUBLANE)], sem).wait()

    @pl.when(i > 0)
    def _():
        drain()

    for s, w in enumerate(words):
        buf[pl.ds(s, tm, stride=SUBLANE), :] = w

    def issue(t, carry):
        src = buf.at[pl.ds(pl.multiple_of(t * SUBLANE, SUBLANE), SUBLANE)]
        for k in range(TOP_K):
            dst = dest_ref[k * n_rows + i * tm + t]
            pltpu.make_async_copy(src, xs_ref.at[pl.ds(pl.multiple_of(dst * SUBLANE, SUBLANE), SUBLANE)], sem).start()
        return carry

    lax.fori_loop(0, tm, issue, 0, unroll=ISSUE_UNROLL)

    @pl.when(i == n_steps - 1)
    def _():
        drain()
        zslab[...] = jnp.zeros_like(zslab)
        n_ranges = fill_lo_ref.shape[0]

        def zero_copy(rr):
            return pltpu.make_async_copy(zslab, xs_ref.at[pl.ds(pl.multiple_of(rr * SUBLANE, SUBLANE), SUBLANE)], zsem)

        def per_range(fn):
            def body(e, carry):
                lax.fori_loop(fill_lo_ref[e], fill_hi_ref[e], lambda rr, c: (fn(rr), c)[1], 0)
                return carry
            lax.fori_loop(0, n_ranges, body, 0)

        per_range(lambda rr: zero_copy(rr).start())
        per_range(lambda rr: zero_copy(rr).wait())


def _dispatch(x1, mod, dest_flat, fill_lo, fill_hi, cap, *, seg_of_row_tile, d):
    r = x1.shape[0]
    slab = _slab_rows(d)
    grid_spec = pltpu.PrefetchScalarGridSpec(
        num_scalar_prefetch=3,
        grid=(r // TM,),
        in_specs=[pl.BlockSpec((TM, d), lambda i, *_: (i, 0)),
                  pl.BlockSpec((1, 6, d), lambda i, *_: (seg_of_row_tile(i, TM), 0, 0))],
        out_specs=pl.BlockSpec(memory_space=pl.ANY),
        scratch_shapes=[pltpu.VMEM((TM * slab, LANE), U32), pltpu.VMEM((slab, LANE), U32),
                        pltpu.SemaphoreType.DMA(()), pltpu.SemaphoreType.DMA(())],
    )
    return pl.pallas_call(
        functools.partial(_dispatch_kernel, n_rows=r),
        grid_spec=grid_spec,
        out_shape=jax.ShapeDtypeStruct((cap * slab, LANE), U32),
        compiler_params=_cparams("arbitrary"),
        name="moe_dispatch",
    )(dest_flat, fill_lo, fill_hi, x1, mod)


def _expert_kernel(blk_e_ref, n_used_ref, xs_ref, wg_ref, wu_ref, wd_ref, o_ref):
    b = pl.program_id(0)
    mb = xs_ref.shape[0] // SUBLANE

    @pl.when(b < n_used_ref[0])
    def _():
        xb = _unpack_rows([xs_ref[pl.ds(s, mb, stride=SUBLANE), :] for s in range(SUBLANE)]).astype(BF16)
        g = jnp.dot(xb, wg_ref[...], preferred_element_type=F32)
        u = jnp.dot(xb, wu_ref[...], preferred_element_type=F32)
        hmid = (g * jax.nn.sigmoid(g) * u).astype(BF16)
        y = jnp.dot(hmid, wd_ref[...], preferred_element_type=F32)
        for s, w in enumerate(_pack_rows(y)):
            o_ref[pl.ds(s, mb, stride=SUBLANE), :] = w

    @pl.when(b >= n_used_ref[0])
    def _():
        o_ref[...] = jnp.zeros_like(o_ref)


def _experts(xs, blk_expert, n_used, wg, wu, wd):
    d, ff = wg.shape[1], wg.shape[2]
    slab = _slab_rows(d)
    n_blk = xs.shape[0] // (MOE_BLOCK * slab)
    xrow = lambda b, be, nu: (jnp.minimum(b, nu[0] - 1), 0)
    grid_spec = pltpu.PrefetchScalarGridSpec(
        num_scalar_prefetch=2,
        grid=(n_blk,),
        in_specs=[pl.BlockSpec((MOE_BLOCK * slab, LANE), xrow),
                  pl.BlockSpec((None, d, ff), lambda b, be, nu: (be[b], 0, 0)),
                  pl.BlockSpec((None, d, ff), lambda b, be, nu: (be[b], 0, 0)),
                  pl.BlockSpec((None, ff, d), lambda b, be, nu: (be[b], 0, 0))],
        out_specs=pl.BlockSpec((MOE_BLOCK * slab, LANE), lambda b, be, nu: (b, 0)),
    )
    return pl.pallas_call(
        _expert_kernel,
        grid_spec=grid_spec,
        out_shape=jax.ShapeDtypeStruct(xs.shape, U32),
        compiler_params=_cparams("arbitrary"),
        name="moe_experts",
    )(blk_expert, n_used, xs, wg, wu, wd)


def _combine_kernel(*refs, n_rows, alpha, emit_h):
    if emit_h:
        dest_ref, x_ref, wt_ref, mod_ref, g_ref, b_ref, modn_ref, ys_ref, o_ref, h_ref, buf, sems = refs
    else:
        dest_ref, x_ref, wt_ref, mod_ref, g_ref, b_ref, ys_ref, o_ref, buf, sems = refs
    i = pl.program_id(0)
    n_steps = pl.num_programs(0)
    tm = x_ref.shape[0]
    slot = i % 2

    def issue_tile(tile, sl):
        def body(t, carry):
            for k in range(TOP_K):
                src = dest_ref[k * n_rows + tile * tm + t]
                pltpu.make_async_copy(ys_ref.at[pl.ds(pl.multiple_of(src * SUBLANE, SUBLANE), SUBLANE)],
                                      buf.at[sl, k, pl.ds(pl.multiple_of(t * SUBLANE, SUBLANE), SUBLANE)],
                                      sems.at[sl]).start()
            return carry
        lax.fori_loop(0, tm, body, 0, unroll=ISSUE_UNROLL)

    @pl.when(i == 0)
    def _():
        issue_tile(0, 0)

    @pl.when(i + 1 < n_steps)
    def _():
        issue_tile(i + 1, 1 - slot)

    def finish(sl):
        for k in range(TOP_K):
            pltpu.make_async_copy(ys_ref.at[pl.ds(0, tm * SUBLANE)], buf.at[sl, k], sems.at[sl]).wait()
        y = [_unpack_rows([buf[sl, k, pl.ds(s, tm, stride=SUBLANE), :] for s in range(SUBLANE)])
             for k in range(TOP_K)]
        f = wt_ref[:, 0:1] * y[0] + wt_ref[:, 1:2] * y[1]
        z = alpha * x_ref[...] + mod_ref[0, 5:6, :] * f
        x2 = _layer_norm(z, g_ref[...], b_ref[...])
        o_ref[...] = x2
        if emit_h:
            h_ref[...] = (x2 * (1.0 + modn_ref[0, 1:2, :]) + modn_ref[0, 0:1, :]).astype(BF16)

    for sl in range(2):
        pl.when(slot == sl)(functools.partial(finish, sl))


def _combine(x1, wt_rows, mod, g, b, dest_flat, ys, mod_next, *, seg_of_row_tile, d, alpha):
    r = x1.shape[0]
    slab = _slab_rows(d)
    emit_h = mod_next is not None
    mod_spec = pl.BlockSpec((1, 6, d), lambda i, *_: (seg_of_row_tile(i, TM), 0, 0))
    row_spec = pl.BlockSpec((TM, d), lambda i, *_: (i, 0))
    vec_spec = pl.BlockSpec((1, d), lambda i, *_: (0, 0))
    in_specs = [row_spec, pl.BlockSpec((TM, TOP_K), lambda i, *_: (i, 0)), mod_spec, vec_spec, vec_spec]
    args = [x1, wt_rows, mod, g.reshape(1, d), b.reshape(1, d)]
    out_specs, out_shape = [row_spec], [jax.ShapeDtypeStruct((r, d), F32)]
    if emit_h:
        in_specs.append(mod_spec)
        args.append(mod_next)
        out_specs.append(row_spec)
        out_shape.append(jax.ShapeDtypeStruct((r, d), BF16))
    in_specs.append(pl.BlockSpec(memory_space=pl.ANY))
    args.append(ys)
    grid_spec = pltpu.PrefetchScalarGridSpec(
        num_scalar_prefetch=1,
        grid=(r // TM,),
        in_specs=in_specs,
        out_specs=out_specs,
        scratch_shapes=[pltpu.VMEM((2, TOP_K, TM * slab, LANE), U32), pltpu.SemaphoreType.DMA((2,))],
    )
    outs = pl.pallas_call(
        functools.partial(_combine_kernel, n_rows=r, alpha=alpha, emit_h=emit_h),
        grid_spec=grid_spec,
        out_shape=out_shape,
        compiler_params=_cparams("arbitrary"),
        name="moe_combine_ln2",
    )(dest_flat, *args)
    return (outs[0], outs[1]) if emit_h else (outs[0], None)


def _rope_tables(n_seq, pad_rows):
    nf = HEAD_DIM // 4
    inv_freq = np.power(np.float32(ROPE_BASE), -np.arange(nf, dtype=np.float32) / np.float32(nf)).astype(np.float32)
    n_grid_rows = n_seq // GRID_W
    ar = (np.arange(n_grid_rows, dtype=np.float32)[:, None] * inv_freq).astype(np.float32)
    ac = (np.arange(GRID_W, dtype=np.float32)[:, None] * inv_freq).astype(np.float32)
    shape = (n_grid_rows, GRID_W, nf)
    by_row = lambda a: jnp.broadcast_to(jnp.asarray(a, F32)[:, None, :], shape)
    by_col = lambda a: jnp.broadcast_to(jnp.asarray(a, F32)[None, :, :], shape)
    cr, sr, cc, sc = np.cos(ar), np.sin(ar), np.cos(ac), np.sin(ac)
    cos_t = jnp.concatenate([by_row(cr), by_row(cr), by_col(cc), by_col(cc)], axis=-1).reshape(n_seq, HEAD_DIM)
    sin_t = jnp.concatenate([by_row(-sr), by_row(sr), by_col(-sc), by_col(sc)], axis=-1).reshape(n_seq, HEAD_DIM)
    cos_t = jnp.concatenate([cos_t, jnp.ones((pad_rows, HEAD_DIM), F32)], axis=0)
    sin_t = jnp.concatenate([sin_t, jnp.zeros((pad_rows, HEAD_DIM), F32)], axis=0)
    return cos_t, sin_t


def kernel(x, c, ctx, c_ctx, w_ada, b_ada, w_in, attn_sink, conv_w, w_attn_proj, w_conv_proj, w_out,
           ln1_g, ln1_b, ln2_g, ln2_b, router_w, router_b, w_gate, w_up, w_down):
    n_batch, n_seq, d = x.shape
    n_ctx = ctx.shape[1]
    depth = w_ada.shape[0]
    n_exp = router_w.shape[1]
    q_width = attn_sink.shape[1] * HEAD_DIM
    in_width = w_in.shape[2]
    kv_width = (in_width - q_width - 5 * d) // 2
    t = n_batch * n_seq
    r = t + n_batch * n_ctx
    alpha = float((2 * depth) ** 0.25)
    assert w_conv_proj.shape[1] == d and n_seq % GRID_W == 0
    assert r % TM_IN == 0 and r % TM == 0 and r % TQ == 0 and t % TM_IN == 0

    def seg_of_row_tile(i, tile):
        return jnp.minimum((i * tile) // n_seq, n_batch)

    x_lat, x_ctx = x.reshape(t, d), ctx.reshape(n_batch * n_ctx, d)
    rope_c, rope_s = _rope_tables(n_seq, TM_IN)
    c_rows = jnp.concatenate([c, c_ctx[None, :], jnp.zeros((8 - n_batch - 1, d), F32)], axis=0)
    mods = [_mod_table(c_rows, w_ada, b_ada[l], l).reshape(8, 6, d) for l in range(depth)]
    rwt = router_w.T
    q_col, k_col, v_col = 5 * d, 5 * d + q_width, 5 * d + q_width + kv_width
    expert_ids = jnp.arange(n_exp, dtype=jnp.int32)

    h = _modulate(x_lat, x_ctx, mods[0], seg_of_row_tile=seg_of_row_tile)
    x_cur = None
    for l in range(depth):
        last = l == depth - 1
        mod = mods[l]
        n_rows = t if last else r
        proj = _in_proj(h, w_in, l, rope_c, rope_s, d=d, q_width=q_width, kv_width=kv_width, n_lat_rows=t,
                        n_seq=n_seq)
        attn = _attention(proj, attn_sink[l], n_rows=n_rows, n_batch=n_batch, n_seq=n_seq, n_ctx=n_ctx, q_col=q_col,
                          k_col=k_col, v_col=v_col, q_width=q_width, kv_width=kv_width)
        src = (x_lat, x_ctx) if l == 0 else (x_cur, None)
        x1 = _merge(src[0], src[1], attn, proj, mod, conv_w[l], w_attn_proj[l].astype(BF16),
                    w_conv_proj[l].astype(BF16), w_out[l].astype(BF16), ln1_g[l], ln1_b[l], n_rows=n_rows,
                    seg_of_row_tile=seg_of_row_tile, d=d, n_seq=n_seq, n_ctx=n_ctx, n_batch=n_batch, alpha=alpha)

        e_idx, wts, rank, cnt = _router(x1, mod, rwt, router_b, seg_of_row_tile=seg_of_row_tile, d=d)
        cap = (n_rows * TOP_K + n_exp * (MOE_BLOCK - 1) + MOE_BLOCK - 1) // MOE_BLOCK * MOE_BLOCK
        n_blk = cap // MOE_BLOCK
        counts = cnt[:, 0].astype(jnp.int32)
        padded = (counts + MOE_BLOCK - 1) // MOE_BLOCK * MOE_BLOCK
        pend = jnp.cumsum(padded)
        pstart = pend - padded
        slot0 = jnp.sum(jnp.where(e_idx[:, :, None] == expert_ids, pstart, 0), axis=-1)
        dest_flat = (slot0 + rank).reshape(-1)
        n_used = (pend[-1] // MOE_BLOCK).astype(jnp.int32)
        blk_start = jnp.minimum(jnp.arange(n_blk, dtype=jnp.int32), n_used - 1) * MOE_BLOCK
        blk_expert = jnp.sum(blk_start[:, None] >= pend[None, :], axis=-1).astype(jnp.int32)
        fill_lo = jnp.concatenate([pstart + counts, pend[-1:]]).astype(jnp.int32)
        fill_hi = jnp.concatenate([pend, jnp.full((1,), cap, pend.dtype)]).astype(jnp.int32)
        xs = _dispatch(x1, mod, dest_flat, fill_lo, fill_hi, cap, seg_of_row_tile=seg_of_row_tile, d=d)
        ys = _experts(xs, blk_expert, n_used.reshape(1), _cast_bf16(w_gate, l), _cast_bf16(w_up, l),
                      _cast_bf16(w_down, l))
        x_cur, h = _combine(x1, wts.T, mod, ln2_g[l], ln2_b[l], dest_flat, ys, None if last else mods[l + 1],
                            seg_of_row_tile=seg_of_row_tile, d=d, alpha=alpha)

    return x_cur.reshape(n_batch, n_seq, d)
```

```python
import functools

import numpy as np
import jax
import jax.numpy as jnp
from jax import lax
from jax.experimental import pallas as pl
from jax.experimental.pallas import tpu as pltpu

F32 = jnp.float32
BF16 = jnp.bfloat16
U32 = jnp.uint32
HIGHEST = lax.Precision.HIGHEST

GRID_W = 64
HEAD_DIM = 128
WINDOW = 128
ROPE_BASE = 10000.0
N_EXPERT_GROUPS = 4
TOP_K = 2
LN_EPS = 1e-5
NEG_INF = -1e30

LANE = 128
SUBLANE = 8
TQ = WINDOW
TM = 256
TM_IN = 512
TN_IN = 1024
TN_CHUNK = 256
TN_MOD = 1024
MOE_BLOCK = 256
ISSUE_UNROLL = 8
VMEM_LIMIT = 56 * 1024 * 1024

NT_DIMS = (((1,), (1,)), ((), ()))
TN_DIMS = (((0,), (0,)), ((), ()))
LOG2E = 1.4426950408889634


def _cparams(*sem):
    return pltpu.CompilerParams(dimension_semantics=sem, vmem_limit_bytes=VMEM_LIMIT)


def _layer_norm(z, g, b):
    mu = jnp.mean(z, axis=-1, keepdims=True)
    zc = z - mu
    var = jnp.mean(zc * zc, axis=-1, keepdims=True)
    return zc * lax.rsqrt(var + LN_EPS) * g + b


def _pack_rows(v):
    half = v.shape[1] // 2
    bits = lax.bitcast_convert_type(v.astype(BF16).astype(F32), U32)
    words = []
    for s in range(half // LANE):
        lo = bits[:, s * LANE:(s + 1) * LANE] >> 16
        hi = bits[:, half + s * LANE:half + (s + 1) * LANE] & np.uint32(0xFFFF0000)
        words.append(lo | hi)
    return words


def _unpack_rows(words):
    lo = [lax.bitcast_convert_type(w << 16, F32) for w in words]
    hi = [lax.bitcast_convert_type(w & np.uint32(0xFFFF0000), F32) for w in words]
    return jnp.concatenate(lo + hi, axis=1)


def _slab_rows(d):
    assert d % (2 * LANE) == 0 and d // (2 * LANE) == SUBLANE, "one row must pack into one (8, 128) uint32 tile"
    return d // (2 * LANE)


def _mod_kernel(c_ref, w_ref, b_ref, o_ref):
    cc = c_ref[...]
    s = cc * jax.nn.sigmoid(cc)
    o_ref[...] = jnp.dot(s, w_ref[...], preferred_element_type=F32, precision=HIGHEST) + b_ref[...]


def _mod_table(c_rows, w_ada, b_ada_l, layer):
    rows, d = c_rows.shape
    n_out = w_ada.shape[2]
    return pl.pallas_call(
        _mod_kernel,
        grid=(n_out // TN_MOD,),
        in_specs=[pl.BlockSpec((rows, d), lambda j: (0, 0)),
                  pl.BlockSpec((None, d, TN_MOD), lambda j: (layer, 0, j)),
                  pl.BlockSpec((1, TN_MOD), lambda j: (0, j))],
        out_specs=pl.BlockSpec((rows, TN_MOD), lambda j: (0, j)),
        out_shape=jax.ShapeDtypeStruct((rows, n_out), F32),
        compiler_params=_cparams("parallel"),
        name="mod_table",
    )(c_rows, w_ada, b_ada_l.reshape(1, n_out))


def _cast_kernel(w_ref, o_ref):
    o_ref[...] = w_ref[...].astype(BF16)


def _cast_bf16(w, layer):
    _, e, a, b = w.shape
    return pl.pallas_call(
        _cast_kernel,
        grid=(e,),
        in_specs=[pl.BlockSpec((None, 1, a, b), lambda i: (layer, i, 0, 0))],
        out_specs=pl.BlockSpec((1, a, b), lambda i: (i, 0, 0)),
        out_shape=jax.ShapeDtypeStruct((e, a, b), BF16),
        compiler_params=_cparams("parallel"),
        name="cast_bf16",
    )(w)


def _modulate_kernel(xa_ref, xb_ref, mod_ref, o_ref, *, n_lat_tiles):
    x = jnp.where(pl.program_id(0) < n_lat_tiles, xa_ref[...], xb_ref[...])
    o_ref[...] = (x * (1.0 + mod_ref[0, 1:2, :]) + mod_ref[0, 0:1, :]).astype(BF16)


def _modulate(x_lat, x_ctx, mod, *, seg_of_row_tile):
    t, d = x_lat.shape
    r = t + x_ctx.shape[0]
    n_lat = t // TM
    return pl.pallas_call(
        functools.partial(_modulate_kernel, n_lat_tiles=n_lat),
        grid=(r // TM,),
        in_specs=[pl.BlockSpec((TM, d), lambda i: (jnp.minimum(i, n_lat - 1), 0)),
                  pl.BlockSpec((TM, d), lambda i: (jnp.maximum(i - n_lat, 0), 0)),
                  pl.BlockSpec((1, 6, d), lambda i: (seg_of_row_tile(i, TM), 0, 0))],
        out_specs=pl.BlockSpec((TM, d), lambda i: (i, 0)),
        out_shape=jax.ShapeDtypeStruct((r, d), BF16),
        compiler_params=_cparams("parallel"),
        name="modulate_in",
    )(x_lat, x_ctx, mod)


def _inproj_kernel(h_ref, w_ref, rc_ref, rs_ref, o_ref, wb_scr, *, n_plain, n_gate, n_q, k_cols, scale):
    j = pl.program_id(0)
    tn = o_ref.shape[1]

    @pl.when(pl.program_id(1) == 0)
    def _():
        wb_scr[...] = w_ref[...].astype(BF16)

    def rope(xh):
        lane = lax.broadcasted_iota(jnp.int32, xh.shape, 1)
        sw = jnp.where((lane & 32) == 0, pltpu.roll(xh, LANE - 32, 1), pltpu.roll(xh, 32, 1))
        return xh * rc_ref[...] + sw * rs_ref[...]

    def column_chunks(epilogue):
        for c0 in range(0, tn, TN_CHUNK):
            acc = jnp.dot(h_ref[...], wb_scr[:, c0:c0 + TN_CHUNK], preferred_element_type=F32)
            for c in range(c0, c0 + TN_CHUNK, LANE):
                o_ref[:, c:c + LANE] = epilogue(acc[:, c - c0:c - c0 + LANE], c).astype(BF16)

    @pl.when(j < n_plain)
    def _():
        column_chunks(lambda a, c: a)

    @pl.when((j >= n_plain) & (j < n_plain + n_gate))
    def _():
        column_chunks(lambda a, c: jax.nn.sigmoid(a))

    @pl.when((j >= n_plain + n_gate) & (j < n_plain + n_gate + n_q))
    def _():
        column_chunks(lambda a, c: rope(a) * scale)

    @pl.when(j == n_plain + n_gate + n_q)
    def _():
        column_chunks(lambda a, c: rope(a) if c < k_cols else a)


def _in_proj(h, w_in, layer, rope_c, rope_s, *, d, q_width, kv_width, n_lat_rows, n_seq):
    r = h.shape[0]
    in_width = w_in.shape[2]
    base = q_width + 2 * kv_width
    assert (3 * d) % TN_IN == 0 and (2 * d) % TN_IN == 0 and q_width % TN_IN == 0 and 2 * kv_width == TN_IN
    n_plain, n_gate, n_q = 3 * d // TN_IN, 2 * d // TN_IN, q_width // TN_IN
    n_j = in_width // TN_IN
    assert n_plain + n_gate + n_q + 1 == n_j and base % TN_IN == 0 and n_seq % TM_IN == 0
    shift = base // TN_IN
    tiles_per_seq = n_seq // TM_IN
    n_lat_tiles = n_lat_rows // TM_IN

    def pos_tile(i):
        return jnp.where(i < n_lat_tiles, i % tiles_per_seq, tiles_per_seq)

    kern = functools.partial(_inproj_kernel, n_plain=n_plain, n_gate=n_gate, n_q=n_q, k_cols=kv_width,
                             scale=HEAD_DIM ** -0.5 * LOG2E)
    return pl.pallas_call(
        kern,
        grid=(n_j, r // TM_IN),
        in_specs=[pl.BlockSpec((TM_IN, d), lambda j, i: (i, 0)),
                  pl.BlockSpec((None, d, TN_IN), lambda j, i: (layer, 0, (j + shift) % n_j)),
                  pl.BlockSpec((TM_IN, LANE), lambda j, i: (pos_tile(i), 0)),
                  pl.BlockSpec((TM_IN, LANE), lambda j, i: (pos_tile(i), 0))],
        out_specs=pl.BlockSpec((TM_IN, TN_IN), lambda j, i: (i, j)),
        out_shape=jax.ShapeDtypeStruct((r, in_width), BF16),
        scratch_shapes=[pltpu.VMEM((d, TN_IN), BF16)],
        compiler_params=_cparams("arbitrary", "arbitrary"),
        name="in_proj",
    )(h, w_in, rope_c, rope_s)


def _attn_kernel(sink_ref, q_ref, kp_ref, kc_ref, kn_ref, vp_ref, vc_ref, vn_ref, kx_ref, vx_ref, o_ref, *,
                 nb_seq, n_lat_tiles, n_kv, group):
    i = pl.program_id(0)
    tq = q_ref.shape[0]
    is_lat = i < n_lat_tiles
    p = i % nb_seq
    has_prev = jnp.logical_and(is_lat, p > 0)
    has_next = jnp.logical_and(is_lat, p < nb_seq - 1)
    n_ctx = kx_ref.shape[0]
    key = lax.broadcasted_iota(jnp.int32, (tq, tq), 0)
    qry = lax.broadcasted_iota(jnp.int32, (tq, tq), 1)
    b_prev = jnp.where(jnp.logical_and(key >= qry, has_prev), 0.0, NEG_INF)
    b_cur = jnp.where(jnp.logical_and(key >= 0, is_lat), 0.0, NEG_INF)
    b_next = jnp.where(jnp.logical_and(key <= qry, has_next), 0.0, NEG_INF)
    bias_t = jnp.concatenate([b_prev, b_cur, b_next, jnp.zeros((n_ctx, tq), F32)], axis=0).astype(BF16)
    qrow = lax.broadcasted_iota(jnp.int32, (group * tq, tq), 0)
    qcol = lax.broadcasted_iota(jnp.int32, (group * tq, tq), 1)
    row_onehot = jnp.where(qrow % tq == qcol, 1.0, 0.0).astype(BF16)

    for h in range(n_kv):
        hs = slice(h * HEAD_DIM, (h + 1) * HEAD_DIM)
        q4 = jnp.concatenate([q_ref[:, (h * group + g) * HEAD_DIM:(h * group + g + 1) * HEAD_DIM]
                              for g in range(group)], axis=0)
        k_all = jnp.concatenate([kp_ref[:, hs], kc_ref[:, hs], kn_ref[:, hs], kx_ref[:, hs]], axis=0)
        v_all = jnp.concatenate([vp_ref[:, hs], vc_ref[:, hs], vn_ref[:, hs], vx_ref[:, hs]], axis=0)
        s_t = lax.dot_general(jnp.concatenate([k_all, bias_t], axis=1), jnp.concatenate([q4, row_onehot], axis=1),
                              NT_DIMS, preferred_element_type=F32)
        p_cols, inv_cols = [], []
        for g in range(group):
            sg = s_t[:, g * tq:(g + 1) * tq]
            sink2 = sink_ref[h * group + g] * LOG2E
            m = jnp.maximum(jnp.max(sg, axis=0, keepdims=True), sink2)
            p = jnp.exp2(sg - m)
            denom = jnp.sum(p, axis=0, keepdims=True) + jnp.exp2(sink2 - m)
            p_cols.append(p.astype(BF16))
            inv_cols.append(1.0 / denom)
        o_t = lax.dot_general(v_all, jnp.concatenate(p_cols, axis=1), TN_DIMS, preferred_element_type=F32)
        o_t = o_t * jnp.concatenate(inv_cols, axis=1)
        for g in range(group):
            hq = h * group + g
            o_ref[:, hq * HEAD_DIM:(hq + 1) * HEAD_DIM] = o_t[:, g * tq:(g + 1) * tq].T.astype(BF16)


def _attention(proj, sink, *, n_rows, n_batch, n_seq, n_ctx, q_col, k_col, v_col, q_width, kv_width):
    t = n_batch * n_seq
    nb_seq = n_seq // TQ
    n_lat_tiles = t // TQ
    n_kv = kv_width // HEAD_DIM
    group = q_width // kv_width
    assert n_ctx % TQ == 0 and q_col % q_width == 0 and k_col % kv_width == 0 and v_col % kv_width == 0
    ctx_tiles = n_ctx // TQ
    kcb, vcb, qcb = k_col // kv_width, v_col // kv_width, q_col // q_width

    def lat(i):
        return i < n_lat_tiles

    def prev_idx(i):
        return jnp.where(jnp.logical_and(lat(i), i % nb_seq > 0), i - 1, i)

    def next_idx(i):
        return jnp.where(jnp.logical_and(lat(i), i % nb_seq < nb_seq - 1), i + 1, i)

    def ctx_idx(i):
        b = jnp.where(lat(i), i // nb_seq, (i - n_lat_tiles) // ctx_tiles)
        return t // n_ctx + b

    assert t % n_ctx == 0
    kern = functools.partial(_attn_kernel, nb_seq=nb_seq, n_lat_tiles=n_lat_tiles, n_kv=n_kv, group=group)
    kv_spec = lambda f, cb: pl.BlockSpec((TQ, kv_width), lambda i: (f(i), cb))
    return pl.pallas_call(
        kern,
        grid=(n_rows // TQ,),
        in_specs=[pl.BlockSpec(memory_space=pltpu.SMEM),
                  pl.BlockSpec((TQ, q_width), lambda i: (i, qcb)),
                  kv_spec(prev_idx, kcb), kv_spec(lambda i: i, kcb), kv_spec(next_idx, kcb),
                  kv_spec(prev_idx, vcb), kv_spec(lambda i: i, vcb), kv_spec(next_idx, vcb),
                  pl.BlockSpec((n_ctx, kv_width), lambda i: (ctx_idx(i), kcb)),
                  pl.BlockSpec((n_ctx, kv_width), lambda i: (ctx_idx(i), vcb))],
        out_specs=pl.BlockSpec((TQ, q_width), lambda i: (i, 0)),
        out_shape=jax.ShapeDtypeStruct((n_rows, q_width), BF16),
        compiler_params=_cparams("parallel"),
        name="window_attention",
    )(sink, proj, proj, proj, proj, proj, proj, proj, proj, proj)


def _merge_kernel(*refs, two_src, tiles_per_seq, n_lat_tiles, alpha):
    if two_src:
        xa_ref, xb_ref = refs[:2]
        refs = refs[2:]
    else:
        xa_ref = refs[0]
        refs = refs[1:]
    (attn_ref, cb_ref, cc_ref, cu_ref, ga_ref, gc_ref, ccp_ref, cup_ref, ccn_ref, cun_ref,
     mod_ref, cw_ref, wa_ref, wc_ref, wo_ref, g_ref, b_ref, o_ref) = refs
    i = pl.program_id(0)
    tm = xa_ref.shape[0]
    hr = ccp_ref.shape[0]
    p = i % tiles_per_seq
    is_lat = i < n_lat_tiles
    first = jnp.where(is_lat, p == 0, True)
    last = jnp.where(is_lat, p == tiles_per_seq - 1, True)
    x = jnp.where(is_lat, xa_ref[...], xb_ref[...]) if two_src else xa_ref[...]

    up = cc_ref[...].astype(F32) * cu_ref[...].astype(F32)
    prev_row = ccp_ref[hr - 1:hr, :].astype(F32) * cup_ref[hr - 1:hr, :].astype(F32)
    next_row = ccn_ref[0:1, :].astype(F32) * cun_ref[0:1, :].astype(F32)
    prev_row = jnp.where(first, 0.0, prev_row)
    next_row = jnp.where(last, 0.0, next_row)
    ridx = lax.broadcasted_iota(jnp.int32, up.shape, 0)
    dn = jnp.where(ridx == 0, prev_row, pltpu.roll(up, 1, 0))
    un = jnp.where(ridx == tm - 1, next_row, pltpu.roll(up, tm - 1, 0))
    conv = cw_ref[0:1, :] * dn + cw_ref[1:2, :] * up + cw_ref[2:3, :] * un
    sconv = (cb_ref[...].astype(F32) * conv).astype(BF16)

    a = jnp.dot(attn_ref[...], wa_ref[...], preferred_element_type=F32)
    s = jnp.dot(sconv, wc_ref[...], preferred_element_type=F32)
    m = (ga_ref[...].astype(F32) * a + gc_ref[...].astype(F32) * s).astype(BF16)
    y = jnp.dot(m, wo_ref[...], preferred_element_type=F32)
    z = alpha * x + mod_ref[0, 2:3, :] * y
    o_ref[...] = _layer_norm(z, g_ref[...], b_ref[...])


def _merge(x_lat, x_ctx, attn, proj, mod, conv_w, wa, wc, wo, g, b, *, n_rows, seg_of_row_tile, d, n_seq, n_ctx,
           n_batch, alpha):
    hr = 16
    assert n_seq % TM == 0 and n_ctx == TM
    tiles_per_seq = n_seq // TM
    n_lat_tiles = n_batch * tiles_per_seq
    two_src = x_ctx is not None
    kern = functools.partial(_merge_kernel, two_src=two_src, tiles_per_seq=tiles_per_seq, n_lat_tiles=n_lat_tiles,
                             alpha=alpha)
    col = lambda cb: pl.BlockSpec((TM, d), lambda i: (i, cb))
    rpt = TM // hr
    n_hblk = proj.shape[0] // hr
    prev = lambda cb: pl.BlockSpec((hr, d), lambda i: (jnp.maximum(i * rpt - 1, 0), cb))
    nxt = lambda cb: pl.BlockSpec((hr, d), lambda i: (jnp.minimum((i + 1) * rpt, n_hblk - 1), cb))
    const = lambda shape: pl.BlockSpec(shape, lambda i: (0,) * len(shape), pipeline_mode=pl.Buffered(1))
    if two_src:
        x_specs = [pl.BlockSpec((TM, d), lambda i: (jnp.minimum(i, n_lat_tiles - 1), 0)),
                   pl.BlockSpec((TM, d), lambda i: (jnp.maximum(i - n_lat_tiles, 0), 0))]
        x_args = (x_lat, x_ctx)
    else:
        x_specs = [pl.BlockSpec((TM, d), lambda i: (i, 0))]
        x_args = (x_lat,)
    return pl.pallas_call(
        kern,
        grid=(n_rows // TM,),
        in_specs=x_specs + [pl.BlockSpec((TM, attn.shape[1]), lambda i: (i, 0)),
                            col(0), col(1), col(2), col(3), col(4),
                            prev(1), prev(2), nxt(1), nxt(2),
                            pl.BlockSpec((1, 6, d), lambda i: (seg_of_row_tile(i, TM), 0, 0)),
                            const((3, d)), const(wa.shape), const(wc.shape), const(wo.shape),
                            const((1, d)), const((1, d))],
        out_specs=pl.BlockSpec((TM, d), lambda i: (i, 0)),
        out_shape=jax.ShapeDtypeStruct((n_rows, d), F32),
        compiler_params=_cparams("parallel"),
        name="merge_ln1",
    )(*x_args, attn, proj, proj, proj, proj, proj, proj, proj, proj, proj, mod, conv_w, wa, wc, wo,
      g.reshape(1, d), b.reshape(1, d))


def _router_kernel(x_ref, mod_ref, rwt_ref, rb_ref, e_ref, w_ref, rank_ref, cnt_ref, cnt_scr, *, n_groups):
    i = pl.program_id(0)
    tm = x_ref.shape[0]
    n_exp = rwt_ref.shape[0]
    per = n_exp // n_groups

    @pl.when(i == 0)
    def _():
        cnt_scr[...] = jnp.zeros_like(cnt_scr)

    h2 = x_ref[...] * (1.0 + mod_ref[0, 4:5, :]) + mod_ref[0, 3:4, :]
    logits = lax.dot_general(rwt_ref[...], h2, NT_DIMS, preferred_element_type=F32, precision=HIGHEST)
    ex = jnp.exp(logits - jnp.max(logits, axis=0, keepdims=True))
    probs = ex / jnp.sum(ex, axis=0, keepdims=True)
    sel3 = (probs + rb_ref[...]).reshape(n_groups, per, tm)

    li = lax.broadcasted_iota(jnp.int32, (n_groups, per, tm), 1).astype(F32)
    m1 = jnp.max(sel3, axis=1, keepdims=True)
    i1 = jnp.min(jnp.where(sel3 == m1, li, float(per)), axis=1, keepdims=True)
    sel3b = jnp.where(li == i1, -jnp.inf, sel3)
    m2 = jnp.max(sel3b, axis=1, keepdims=True)
    i2 = jnp.min(jnp.where(sel3b == m2, li, float(per)), axis=1, keepdims=True)
    score = m1 + m2
    gi = lax.broadcasted_iota(jnp.int32, (n_groups, 1, tm), 0).astype(F32)
    gbest = jnp.min(jnp.where(score == jnp.max(score, axis=0, keepdims=True), gi, float(n_groups)),
                    axis=0, keepdims=True)
    pick = gi == gbest
    l1 = jnp.sum(jnp.where(pick, i1, 0.0), axis=0)
    l2 = jnp.sum(jnp.where(pick, i2, 0.0), axis=0)
    e1 = gbest[0] * per + l1
    e2 = gbest[0] * per + l2

    ei = lax.broadcasted_iota(jnp.int32, (n_exp, tm), 0).astype(F32)
    is1 = ei == e1
    is2 = ei == e2
    p1 = jnp.sum(jnp.where(is1, probs, 0.0), axis=0, keepdims=True)
    p2 = jnp.sum(jnp.where(is2, probs, 0.0), axis=0, keepdims=True)
    psum = p1 + p2

    onehot = jnp.where(jnp.logical_or(is1, is2), 1.0, 0.0)
    srow = lax.broadcasted_iota(jnp.int32, (tm, tm), 0)
    scol = lax.broadcasted_iota(jnp.int32, (tm, tm), 1)
    before = jnp.where(srow < scol, 1.0, 0.0).astype(BF16)
    excl = jnp.dot(onehot.astype(BF16), before, preferred_element_type=F32) + cnt_scr[:, 0:1]
    r1 = jnp.sum(jnp.where(is1, excl, 0.0), axis=0, keepdims=True)
    r2 = jnp.sum(jnp.where(is2, excl, 0.0), axis=0, keepdims=True)
    cnt_scr[...] = cnt_scr[...] + jnp.sum(onehot, axis=1, keepdims=True)

    e_ref[0:1, :] = e1.astype(jnp.int32)
    e_ref[1:2, :] = e2.astype(jnp.int32)
    w_ref[0:1, :] = p1 / psum
    w_ref[1:2, :] = p2 / psum
    rank_ref[0:1, :] = r1.astype(jnp.int32)
    rank_ref[1:2, :] = r2.astype(jnp.int32)
    cnt_ref[...] = cnt_scr[...]


def _router(x1, mod, rwt, rb, *, seg_of_row_tile, d):
    r = x1.shape[0]
    n_exp = rwt.shape[0]
    row2 = lambda dt: jax.ShapeDtypeStruct((TOP_K, r), dt)
    spec2 = pl.BlockSpec((TOP_K, TM), lambda i: (0, i))
    return pl.pallas_call(
        functools.partial(_router_kernel, n_groups=N_EXPERT_GROUPS),
        grid=(r // TM,),
        in_specs=[pl.BlockSpec((TM, d), lambda i: (i, 0)),
                  pl.BlockSpec((1, 6, d), lambda i: (seg_of_row_tile(i, TM), 0, 0)),
                  pl.BlockSpec((n_exp, d), lambda i: (0, 0)),
                  pl.BlockSpec((n_exp, 1), lambda i: (0, 0))],
        out_specs=[spec2, spec2, spec2, pl.BlockSpec((n_exp, LANE), lambda i: (0, 0))],
        out_shape=[row2(jnp.int32), row2(F32), row2(jnp.int32), jax.ShapeDtypeStruct((n_exp, LANE), F32)],
        scratch_shapes=[pltpu.VMEM((n_exp, LANE), F32)],
        compiler_params=_cparams("arbitrary"),
        name="router",
    )(x1, mod, rwt, rb.reshape(n_exp, 1))


def _dispatch_kernel(dest_ref, fill_lo_ref, fill_hi_ref, x_ref, mod_ref, xs_ref, buf, zslab, sem, zsem, *, n_rows):
    i = pl.program_id(0)
    n_steps = pl.num_programs(0)
    tm = x_ref.shape[0]
    words = _pack_rows(x_ref[...] * (1.0 + mod_ref[0, 4:5, :]) + mod_ref[0, 3:4, :])

    def drain():
        for _ in range(TOP_K):
            pltpu.make_async_copy(buf, xs_ref.at[pl.ds(0, tm * SUBLANE)], sem).wait()

    @pl.when(i > 0)
    def _():
        drain()

    for s, w in enumerate(words):
        buf[pl.ds(s, tm, stride=SUBLANE), :] = w

    def issue(t, carry):
        src = buf.at[pl.ds(pl.multiple_of(t * SUBLANE, SUBLANE), SUBLANE)]
        for k in range(TOP_K):
            dst = dest_ref[k * n_rows + i * tm + t]
            pltpu.make_async_copy(src, xs_ref.at[pl.ds(pl.multiple_of(dst * SUBLANE, SUBLANE), SUBLANE)],
                                  sem).start(priority=k % 2)
        return carry

    lax.fori_loop(0, tm, issue, 0, unroll=ISSUE_UNROLL)

    @pl.when(i == n_steps - 1)
    def _():
        drain()
        zslab[...] = jnp.zeros_like(zslab)
        n_ranges = fill_lo_ref.shape[0]

        def zero_copy(rr):
            return pltpu.make_async_copy(zslab, xs_ref.at[pl.ds(pl.multiple_of(rr * SUBLANE, SUBLANE), SUBLANE)], zsem)

        def per_range(fn):
            def body(e, carry):
                lax.fori_loop(fill_lo_ref[e], fill_hi_ref[e], lambda rr, c: (fn(rr), c)[1], 0)
                return carry
            lax.fori_loop(0, n_ranges, body, 0)

        per_range(lambda rr: zero_copy(rr).start())
        per_range(lambda rr: zero_copy(rr).wait())


def _dispatch(x1, mod, dest_flat, fill_lo, fill_hi, cap, *, seg_of_row_tile, d):
    r = x1.shape[0]
    slab = _slab_rows(d)
    grid_spec = pltpu.PrefetchScalarGridSpec(
        num_scalar_prefetch=3,
        grid=(r // TM,),
        in_specs=[pl.BlockSpec((TM, d), lambda i, *_: (i, 0)),
                  pl.BlockSpec((1, 6, d), lambda i, *_: (seg_of_row_tile(i, TM), 0, 0))],
        out_specs=pl.BlockSpec(memory_space=pl.ANY),
        scratch_shapes=[pltpu.VMEM((TM * slab, LANE), U32), pltpu.VMEM((slab, LANE), U32),
                        pltpu.SemaphoreType.DMA(()), pltpu.SemaphoreType.DMA(())],
    )
    return pl.pallas_call(
        functools.partial(_dispatch_kernel, n_rows=r),
        grid_spec=grid_spec,
        out_shape=jax.ShapeDtypeStruct((cap * slab, LANE), U32),
        compiler_params=_cparams("arbitrary"),
        name="moe_dispatch",
    )(dest_flat, fill_lo, fill_hi, x1, mod)


def _expert_kernel(blk_e_ref, n_used_ref, xs_ref, wg_ref, wu_ref, wd_ref, o_ref):
    b = pl.program_id(0)
    mb = xs_ref.shape[0] // SUBLANE

    @pl.when(b < n_used_ref[0])
    def _():
        xb = _unpack_rows([xs_ref[pl.ds(s, mb, stride=SUBLANE), :] for s in range(SUBLANE)]).astype(BF16)
        g = jnp.dot(xb, wg_ref[...], preferred_element_type=F32)
        u = jnp.dot(xb, wu_ref[...], preferred_element_type=F32)
        hmid = (g * jax.nn.sigmoid(g) * u).astype(BF16)
        y = jnp.dot(hmid, wd_ref[...], preferred_element_type=F32)
        for s, w in enumerate(_pack_rows(y)):
            o_ref[pl.ds(s, mb, stride=SUBLANE), :] = w

    @pl.when(b >= n_used_ref[0])
    def _():
        o_ref[...] = jnp.zeros_like(o_ref)


def _experts(xs, blk_expert, n_used, wg, wu, wd):
    d, ff = wg.shape[1], wg.shape[2]
    slab = _slab_rows(d)
    n_blk = xs.shape[0] // (MOE_BLOCK * slab)
    xrow = lambda b, be, nu: (jnp.minimum(b, nu[0] - 1), 0)
    grid_spec = pltpu.PrefetchScalarGridSpec(
        num_scalar_prefetch=2,
        grid=(n_blk,),
        in_specs=[pl.BlockSpec((MOE_BLOCK * slab, LANE), xrow),
                  pl.BlockSpec((None, d, ff), lambda b, be, nu: (be[b], 0, 0)),
                  pl.BlockSpec((None, d, ff), lambda b, be, nu: (be[b], 0, 0)),
                  pl.BlockSpec((None, ff, d), lambda b, be, nu: (be[b], 0, 0))],
        out_specs=pl.BlockSpec((MOE_BLOCK * slab, LANE), lambda b, be, nu: (b, 0)),
    )
    return pl.pallas_call(
        _expert_kernel,
        grid_spec=grid_spec,
        out_shape=jax.ShapeDtypeStruct(xs.shape, U32),
        compiler_params=_cparams("arbitrary"),
        name="moe_experts",
    )(blk_expert, n_used, xs, wg, wu, wd)


def _combine_kernel(*refs, n_rows, alpha, emit_h):
    if emit_h:
        dest_ref, x_ref, wt_ref, mod_ref, g_ref, b_ref, modn_ref, ys_ref, o_ref, h_ref, buf, sems = refs
    else:
        dest_ref, x_ref, wt_ref, mod_ref, g_ref, b_ref, ys_ref, o_ref, buf, sems = refs
    i = pl.program_id(0)
    n_steps = pl.num_programs(0)
    tm = x_ref.shape[0]
    slot = i % 2

    def issue_tile(tile, sl):
        def body(t, carry):
            for k in range(TOP_K):
                src = dest_ref[k * n_rows + tile * tm + t]
                pltpu.make_async_copy(ys_ref.at[pl.ds(pl.multiple_of(src * SUBLANE, SUBLANE), SUBLANE)],
                                      buf.at[sl, k, pl.ds(pl.multiple_of(t * SUBLANE, SUBLANE), SUBLANE)],
                                      sems.at[sl]).start(priority=k % 2)
            return carry
        lax.fori_loop(0, tm, body, 0, unroll=ISSUE_UNROLL)

    @pl.when(i == 0)
    def _():
        issue_tile(0, 0)

    @pl.when(i + 1 < n_steps)
    def _():
        issue_tile(i + 1, 1 - slot)

    def finish(sl):
        for k in range(TOP_K):
            pltpu.make_async_copy(ys_ref.at[pl.ds(0, tm * SUBLANE)], buf.at[sl, k], sems.at[sl]).wait()
        y = [_unpack_rows([buf[sl, k, pl.ds(s, tm, stride=SUBLANE), :] for s in range(SUBLANE)])
             for k in range(TOP_K)]
        f = wt_ref[:, 0:1] * y[0] + wt_ref[:, 1:2] * y[1]
        z = alpha * x_ref[...] + mod_ref[0, 5:6, :] * f
        x2 = _layer_norm(z, g_ref[...], b_ref[...])
        o_ref[...] = x2
        if emit_h:
            h_ref[...] = (x2 * (1.0 + modn_ref[0, 1:2, :]) + modn_ref[0, 0:1, :]).astype(BF16)

    for sl in range(2):
        pl.when(slot == sl)(functools.partial(finish, sl))


def _combine(x1, wt_rows, mod, g, b, dest_flat, ys, mod_next, *, seg_of_row_tile, d, alpha):
    r = x1.shape[0]
    slab = _slab_rows(d)
    emit_h = mod_next is not None
    mod_spec = pl.BlockSpec((1, 6, d), lambda i, *_: (seg_of_row_tile(i, TM), 0, 0))
    row_spec = pl.BlockSpec((TM, d), lambda i, *_: (i, 0))
    vec_spec = pl.BlockSpec((1, d), lambda i, *_: (0, 0))
    in_specs = [row_spec, pl.BlockSpec((TM, TOP_K), lambda i, *_: (i, 0)), mod_spec, vec_spec, vec_spec]
    args = [x1, wt_rows, mod, g.reshape(1, d), b.reshape(1, d)]
    out_specs, out_shape = [row_spec], [jax.ShapeDtypeStruct((r, d), F32)]
    if emit_h:
        in_specs.append(mod_spec)
        args.append(mod_next)
        out_specs.append(row_spec)
        out_shape.append(jax.ShapeDtypeStruct((r, d), BF16))
    in_specs.append(pl.BlockSpec(memory_space=pl.ANY))
    args.append(ys)
    grid_spec = pltpu.PrefetchScalarGridSpec(
        num_scalar_prefetch=1,
        grid=(r // TM,),
        in_specs=in_specs,
        out_specs=out_specs,
        scratch_shapes=[pltpu.VMEM((2, TOP_K, TM * slab, LANE), U32), pltpu.SemaphoreType.DMA((2,))],
    )
    outs = pl.pallas_call(
        functools.partial(_combine_kernel, n_rows=r, alpha=alpha, emit_h=emit_h),
        grid_spec=grid_spec,
        out_shape=out_shape,
        compiler_params=_cparams("arbitrary"),
        name="moe_combine_ln2",
    )(dest_flat, *args)
    return (outs[0], outs[1]) if emit_h else (outs[0], None)


def _rope_tables(n_seq, pad_rows):
    nf = HEAD_DIM // 4
    inv_freq = np.power(np.float32(ROPE_BASE), -np.arange(nf, dtype=np.float32) / np.float32(nf)).astype(np.float32)
    n_grid_rows = n_seq // GRID_W
    ar = (np.arange(n_grid_rows, dtype=np.float32)[:, None] * inv_freq).astype(np.float32)
    ac = (np.arange(GRID_W, dtype=np.float32)[:, None] * inv_freq).astype(np.float32)
    shape = (n_grid_rows, GRID_W, nf)
    by_row = lambda a: jnp.broadcast_to(jnp.asarray(a, F32)[:, None, :], shape)
    by_col = lambda a: jnp.broadcast_to(jnp.asarray(a, F32)[None, :, :], shape)
    cr, sr, cc, sc = np.cos(ar), np.sin(ar), np.cos(ac), np.sin(ac)
    cos_t = jnp.concatenate([by_row(cr), by_row(cr), by_col(cc), by_col(cc)], axis=-1).reshape(n_seq, HEAD_DIM)
    sin_t = jnp.concatenate([by_row(-sr), by_row(sr), by_col(-sc), by_col(sc)], axis=-1).reshape(n_seq, HEAD_DIM)
    cos_t = jnp.concatenate([cos_t, jnp.ones((pad_rows, HEAD_DIM), F32)], axis=0)
    sin_t = jnp.concatenate([sin_t, jnp.zeros((pad_rows, HEAD_DIM), F32)], axis=0)
    return cos_t, sin_t


def kernel(x, c, ctx, c_ctx, w_ada, b_ada, w_in, attn_sink, conv_w, w_attn_proj, w_conv_proj, w_out,
           ln1_g, ln1_b, ln2_g, ln2_b, router_w, router_b, w_gate, w_up, w_down):
    n_batch, n_seq, d = x.shape
    n_ctx = ctx.shape[1]
    depth = w_ada.shape[0]
    n_exp = router_w.shape[1]
    q_width = attn_sink.shape[1] * HEAD_DIM
    in_width = w_in.shape[2]
    kv_width = (in_width - q_width - 5 * d) // 2
    t = n_batch * n_seq
    r = t + n_batch * n_ctx
    alpha = float((2 * depth) ** 0.25)
    assert w_conv_proj.shape[1] == d and n_seq % GRID_W == 0
    assert r % TM_IN == 0 and r % TM == 0 and r % TQ == 0 and t % TM_IN == 0

    def seg_of_row_tile(i, tile):
        return jnp.minimum((i * tile) // n_seq, n_batch)

    x_lat, x_ctx = x.reshape(t, d), ctx.reshape(n_batch * n_ctx, d)
    rope_c, rope_s = _rope_tables(n_seq, TM_IN)
    c_rows = jnp.concatenate([c, c_ctx[None, :], jnp.zeros((8 - n_batch - 1, d), F32)], axis=0)
    mods = [_mod_table(c_rows, w_ada, b_ada[l], l).reshape(8, 6, d) for l in range(depth)]
    rwt = router_w.T
    q_col, k_col, v_col = 5 * d, 5 * d + q_width, 5 * d + q_width + kv_width
    expert_ids = jnp.arange(n_exp, dtype=jnp.int32)

    h = _modulate(x_lat, x_ctx, mods[0], seg_of_row_tile=seg_of_row_tile)
    x_cur = None
    for l in range(depth):
        last = l == depth - 1
        mod = mods[l]
        n_rows = t if last else r
        proj = _in_proj(h, w_in, l, rope_c, rope_s, d=d, q_width=q_width, kv_width=kv_width, n_lat_rows=t,
                        n_seq=n_seq)
        attn = _attention(proj, attn_sink[l], n_rows=n_rows, n_batch=n_batch, n_seq=n_seq, n_ctx=n_ctx, q_col=q_col,
                          k_col=k_col, v_col=v_col, q_width=q_width, kv_width=kv_width)
        src = (x_lat, x_ctx) if l == 0 else (x_cur, None)
        x1 = _merge(src[0], src[1], attn, proj, mod, conv_w[l], w_attn_proj[l].astype(BF16),
                    w_conv_proj[l].astype(BF16), w_out[l].astype(BF16), ln1_g[l], ln1_b[l], n_rows=n_rows,
                    seg_of_row_tile=seg_of_row_tile, d=d, n_seq=n_seq, n_ctx=n_ctx, n_batch=n_batch, alpha=alpha)

        e_idx, wts, rank, cnt = _router(x1, mod, rwt, router_b, seg_of_row_tile=seg_of_row_tile, d=d)
        cap = (n_rows * TOP_K + n_exp * (MOE_BLOCK - 1) + MOE_BLOCK - 1) // MOE_BLOCK * MOE_BLOCK
        n_blk = cap // MOE_BLOCK
        counts = cnt[:, 0].astype(jnp.int32)
        padded = (counts + MOE_BLOCK - 1) // MOE_BLOCK * MOE_BLOCK
        pend = jnp.cumsum(padded)
        pstart = pend - padded
        slot0 = jnp.sum(jnp.where(e_idx[:, :, None] == expert_ids, pstart, 0), axis=-1)
        dest_flat = (slot0 + rank).reshape(-1)
        n_used = (pend[-1] // MOE_BLOCK).astype(jnp.int32)
        blk_start = jnp.minimum(jnp.arange(n_blk, dtype=jnp.int32), n_used - 1) * MOE_BLOCK
        blk_expert = jnp.sum(blk_start[:, None] >= pend[None, :], axis=-1).astype(jnp.int32)
        fill_lo = jnp.concatenate([pstart + counts, pend[-1:]]).astype(jnp.int32)
        fill_hi = jnp.concatenate([pend, jnp.full((1,), cap, pend.dtype)]).astype(jnp.int32)
        xs = _dispatch(x1, mod, dest_flat, fill_lo, fill_hi, cap, seg_of_row_tile=seg_of_row_tile, d=d)
        ys = _experts(xs, blk_expert, n_used.reshape(1), _cast_bf16(w_gate, l), _cast_bf16(w_up, l),
                      _cast_bf16(w_down, l))
        x_cur, h = _combine(x1, wts.T, mod, ln2_g[l], ln2_b[l], dest_flat, ys, None if last else mods[l + 1],
                            seg_of_row_tile=seg_of_row_tile, d=d, alpha=alpha)

    return x_cur.reshape(n_batch, n_seq, d)
```

```python
import functools

import numpy as np
import jax
import jax.numpy as jnp
from jax import lax
from jax.experimental import pallas as pl
from jax.experimental.pallas import tpu as pltpu

F32 = jnp.float32
BF16 = jnp.bfloat16
U32 = jnp.uint32
HIGHEST = lax.Precision.HIGHEST

GRID_W = 64
HEAD_DIM = 128
WINDOW = 128
ROPE_BASE = 10000.0
N_EXPERT_GROUPS = 4
TOP_K = 2
LN_EPS = 1e-5
NEG_INF = -1e30

LANE = 128
SUBLANE = 8
TQ = WINDOW
TM = 256
TM_IN = 1040
TN_IN = 1024
TN_CHUNK = 256
EXPERT_CAST_BLOCKS = 256
TM_ROUTER = 512
TN_MOD = 1024
MOE_BLOCK = 256
ISSUE_UNROLL = 8
VMEM_LIMIT = 56 * 1024 * 1024

NT_DIMS = (((1,), (1,)), ((), ()))
TN_DIMS = (((0,), (0,)), ((), ()))
LOG2E = 1.4426950408889634


def _cparams(*sem):
    return pltpu.CompilerParams(dimension_semantics=sem, vmem_limit_bytes=VMEM_LIMIT)


def _layer_norm(z, g, b):
    mu = jnp.mean(z, axis=-1, keepdims=True)
    zc = z - mu
    var = jnp.mean(zc * zc, axis=-1, keepdims=True)
    return zc * lax.rsqrt(var + LN_EPS) * g + b


def _pack_rows(v):
    half = v.shape[1] // 2
    bits = lax.bitcast_convert_type(v.astype(BF16).astype(F32), U32)
    words = []
    for s in range(half // LANE):
        lo = bits[:, s * LANE:(s + 1) * LANE] >> 16
        hi = bits[:, half + s * LANE:half + (s + 1) * LANE] & np.uint32(0xFFFF0000)
        words.append(lo | hi)
    return words


def _unpack_rows(words):
    lo = [lax.bitcast_convert_type(w << 16, F32) for w in words]
    hi = [lax.bitcast_convert_type(w & np.uint32(0xFFFF0000), F32) for w in words]
    return jnp.concatenate(lo + hi, axis=1)


def _slab_rows(d):
    assert d % (2 * LANE) == 0 and d // (2 * LANE) == SUBLANE, "one row must pack into one (8, 128) uint32 tile"
    return d // (2 * LANE)


def _mod_kernel(c_ref, w_ref, b_ref, o_ref):
    cc = c_ref[...]
    s = cc * jax.nn.sigmoid(cc)
    o_ref[...] = jnp.dot(s, w_ref[...], preferred_element_type=F32, precision=HIGHEST) + b_ref[...]


def _mod_table(c_rows, w_ada, b_ada_l, layer):
    rows, d = c_rows.shape
    n_out = w_ada.shape[2]
    return pl.pallas_call(
        _mod_kernel,
        grid=(n_out // TN_MOD,),
        in_specs=[pl.BlockSpec((rows, d), lambda j: (0, 0)),
                  pl.BlockSpec((None, d, TN_MOD), lambda j: (layer, 0, j)),
                  pl.BlockSpec((1, TN_MOD), lambda j: (0, j))],
        out_specs=pl.BlockSpec((rows, TN_MOD), lambda j: (0, j)),
        out_shape=jax.ShapeDtypeStruct((rows, n_out), F32),
        compiler_params=_cparams("parallel"),
        name="mod_table",
    )(c_rows, w_ada, b_ada_l.reshape(1, n_out))


def _modulate_kernel(xa_ref, xb_ref, mod_ref, o_ref, *, n_lat_tiles):
    x = jnp.where(pl.program_id(0) < n_lat_tiles, xa_ref[...], xb_ref[...])
    o_ref[...] = (x * (1.0 + mod_ref[0, 1:2, :]) + mod_ref[0, 0:1, :]).astype(BF16)


def _modulate(x_lat, x_ctx, mod, *, seg_of_row_tile):
    t, d = x_lat.shape
    r = t + x_ctx.shape[0]
    n_lat = t // TM
    return pl.pallas_call(
        functools.partial(_modulate_kernel, n_lat_tiles=n_lat),
        grid=(r // TM,),
        in_specs=[pl.BlockSpec((TM, d), lambda i: (jnp.minimum(i, n_lat - 1), 0)),
                  pl.BlockSpec((TM, d), lambda i: (jnp.maximum(i - n_lat, 0), 0)),
                  pl.BlockSpec((1, 6, d), lambda i: (seg_of_row_tile(i, TM), 0, 0))],
        out_specs=pl.BlockSpec((TM, d), lambda i: (i, 0)),
        out_shape=jax.ShapeDtypeStruct((r, d), BF16),
        compiler_params=_cparams("parallel"),
        name="modulate_in",
    )(x_lat, x_ctx, mod)


def _inproj_kernel(h_ref, w_ref, rc_ref, rs_ref, eg_ref, eu_ref, ed_ref, o_ref, ego_ref, euo_ref, edo_ref, wb_scr, *,
                   n_plain, n_gate, n_q, k_cols, scale):
    j = pl.program_id(0)
    tn = o_ref.shape[1]

    @pl.when(pl.program_id(1) == 0)
    def _():
        wb_scr[...] = w_ref[...].astype(BF16)

    def rope(xh):
        lane = lax.broadcasted_iota(jnp.int32, xh.shape, 1)
        sw = jnp.where((lane & 32) == 0, pltpu.roll(xh, LANE - 32, 1), pltpu.roll(xh, 32, 1))
        return xh * rc_ref[...] + sw * rs_ref[...]

    def column_chunks(epilogue):
        ego_ref[...] = eg_ref[...].astype(BF16)
        euo_ref[...] = eu_ref[...].astype(BF16)
        edo_ref[...] = ed_ref[...].astype(BF16)
        for c0 in range(0, tn, TN_CHUNK):
            acc = jnp.dot(h_ref[...], wb_scr[:, c0:c0 + TN_CHUNK], preferred_element_type=F32)
            for c in range(c0, c0 + TN_CHUNK, LANE):
                o_ref[:, c:c + LANE] = epilogue(acc[:, c - c0:c - c0 + LANE], c).astype(BF16)

    @pl.when(j < n_plain)
    def _():
        column_chunks(lambda a, c: a)

    @pl.when((j >= n_plain) & (j < n_plain + n_gate))
    def _():
        column_chunks(lambda a, c: jax.nn.sigmoid(a))

    @pl.when((j >= n_plain + n_gate) & (j < n_plain + n_gate + n_q))
    def _():
        column_chunks(lambda a, c: rope(a) * scale)

    @pl.when(j == n_plain + n_gate + n_q)
    def _():
        column_chunks(lambda a, c: rope(a) if c < k_cols else a)


def _in_proj(h, w_in, layer, rope_c, rope_s, w_gate, w_up, w_down, *, d, q_width, kv_width):
    r = h.shape[0]
    in_width = w_in.shape[2]
    base = q_width + 2 * kv_width
    assert (3 * d) % TN_IN == 0 and (2 * d) % TN_IN == 0 and q_width % TN_IN == 0 and 2 * kv_width == TN_IN
    n_plain, n_gate, n_q = 3 * d // TN_IN, 2 * d // TN_IN, q_width // TN_IN
    n_j = in_width // TN_IN
    assert n_plain + n_gate + n_q + 1 == n_j and base % TN_IN == 0 and r % TM_IN == 0
    shift = base // TN_IN

    kern = functools.partial(_inproj_kernel, n_plain=n_plain, n_gate=n_gate, n_q=n_q, k_cols=kv_width,
                             scale=HEAD_DIM ** -0.5 * LOG2E)
    n_i = r // TM_IN
    depth, n_exp, _, ff = w_gate.shape
    assert n_j * n_i >= EXPERT_CAST_BLOCKS
    rows_gu, rows_d = n_exp * d // EXPERT_CAST_BLOCKS, n_exp * ff // EXPERT_CAST_BLOCKS
    assert rows_gu * EXPERT_CAST_BLOCKS == n_exp * d and rows_d * EXPERT_CAST_BLOCKS == n_exp * ff
    assert rows_gu % 16 == 0 and rows_d % 16 == 0

    def cast_blk(j, i):
        return jnp.minimum(j * n_i + i, EXPERT_CAST_BLOCKS - 1)

    cast_in = lambda rows, cols: pl.BlockSpec((None, rows, cols), lambda j, i: (layer, cast_blk(j, i), 0))
    cast_out = lambda rows, cols: pl.BlockSpec((rows, cols), lambda j, i: (cast_blk(j, i), 0))
    proj, wg_b, wu_b, wd_b = pl.pallas_call(
        kern,
        grid=(n_j, n_i),
        in_specs=[pl.BlockSpec((TM_IN, d), lambda j, i: (i, 0)),
                  pl.BlockSpec((None, d, TN_IN), lambda j, i: (layer, 0, (j + shift) % n_j)),
                  pl.BlockSpec((TM_IN, LANE), lambda j, i: (i, 0)),
                  pl.BlockSpec((TM_IN, LANE), lambda j, i: (i, 0)),
                  cast_in(rows_gu, ff), cast_in(rows_gu, ff), cast_in(rows_d, d)],
        out_specs=[pl.BlockSpec((TM_IN, TN_IN), lambda j, i: (i, j)),
                   cast_out(rows_gu, ff), cast_out(rows_gu, ff), cast_out(rows_d, d)],
        out_shape=[jax.ShapeDtypeStruct((r, in_width), BF16),
                   jax.ShapeDtypeStruct((n_exp * d, ff), BF16), jax.ShapeDtypeStruct((n_exp * d, ff), BF16),
                   jax.ShapeDtypeStruct((n_exp * ff, d), BF16)],
        scratch_shapes=[pltpu.VMEM((d, TN_IN), BF16)],
        compiler_params=_cparams("arbitrary", "arbitrary"),
        name="in_proj",
    )(h, w_in, rope_c, rope_s, w_gate.reshape(depth, n_exp * d, ff), w_up.reshape(depth, n_exp * d, ff),
      w_down.reshape(depth, n_exp * ff, d))
    return proj, wg_b.reshape(n_exp, d, ff), wu_b.reshape(n_exp, d, ff), wd_b.reshape(n_exp, ff, d)


def _attn_kernel(sink_ref, q_ref, kp_ref, kc_ref, kn_ref, vp_ref, vc_ref, vn_ref, kx_ref, vx_ref, o_ref, *,
                 nb_seq, n_lat_tiles, n_kv, group):
    i = pl.program_id(0)
    tq = q_ref.shape[0]
    is_lat = i < n_lat_tiles
    p = i % nb_seq
    has_prev = jnp.logical_and(is_lat, p > 0)
    has_next = jnp.logical_and(is_lat, p < nb_seq - 1)
    n_ctx = kx_ref.shape[0]
    key = lax.broadcasted_iota(jnp.int32, (tq, tq), 0)
    qry = lax.broadcasted_iota(jnp.int32, (tq, tq), 1)
    b_prev = jnp.where(jnp.logical_and(key >= qry, has_prev), 0.0, NEG_INF)
    b_cur = jnp.where(jnp.logical_and(key >= 0, is_lat), 0.0, NEG_INF)
    b_next = jnp.where(jnp.logical_and(key <= qry, has_next), 0.0, NEG_INF)
    bias_t = jnp.concatenate([b_prev, b_cur, b_next, jnp.zeros((n_ctx, tq), F32)], axis=0).astype(BF16)
    qrow = lax.broadcasted_iota(jnp.int32, (group * tq, tq), 0)
    qcol = lax.broadcasted_iota(jnp.int32, (group * tq, tq), 1)
    row_onehot = jnp.where(qrow % tq == qcol, 1.0, 0.0).astype(BF16)

    for h in range(n_kv):
        hs = slice(h * HEAD_DIM, (h + 1) * HEAD_DIM)
        q4 = jnp.concatenate([q_ref[:, (h * group + g) * HEAD_DIM:(h * group + g + 1) * HEAD_DIM]
                              for g in range(group)], axis=0)
        k_all = jnp.concatenate([kp_ref[:, hs], kc_ref[:, hs], kn_ref[:, hs], kx_ref[:, hs]], axis=0)
        v_all = jnp.concatenate([vp_ref[:, hs], vc_ref[:, hs], vn_ref[:, hs], vx_ref[:, hs]], axis=0)
        s_t = lax.dot_general(jnp.concatenate([k_all, bias_t], axis=1), jnp.concatenate([q4, row_onehot], axis=1),
                              NT_DIMS, preferred_element_type=F32)
        p_cols, inv_cols = [], []
        for g in range(group):
            sg = s_t[:, g * tq:(g + 1) * tq]
            sink2 = sink_ref[h * group + g] * LOG2E
            m = jnp.maximum(jnp.max(sg, axis=0, keepdims=True), sink2)
            p = jnp.exp2(sg - m)
            denom = jnp.sum(p, axis=0, keepdims=True) + jnp.exp2(sink2 - m)
            p_cols.append(p.astype(BF16))
            inv_cols.append(1.0 / denom)
        o_t = lax.dot_general(v_all, jnp.concatenate(p_cols, axis=1), TN_DIMS, preferred_element_type=F32)
        o_t = o_t * jnp.concatenate(inv_cols, axis=1)
        for g in range(group):
            hq = h * group + g
            o_ref[:, hq * HEAD_DIM:(hq + 1) * HEAD_DIM] = o_t[:, g * tq:(g + 1) * tq].T.astype(BF16)


def _attention(proj, sink, *, n_rows, n_batch, n_seq, n_ctx, q_col, k_col, v_col, q_width, kv_width):
    t = n_batch * n_seq
    nb_seq = n_seq // TQ
    n_lat_tiles = t // TQ
    n_kv = kv_width // HEAD_DIM
    group = q_width // kv_width
    assert n_ctx % TQ == 0 and q_col % q_width == 0 and k_col % kv_width == 0 and v_col % kv_width == 0
    ctx_tiles = n_ctx // TQ
    kcb, vcb, qcb = k_col // kv_width, v_col // kv_width, q_col // q_width

    def lat(i):
        return i < n_lat_tiles

    def prev_idx(i):
        return jnp.where(jnp.logical_and(lat(i), i % nb_seq > 0), i - 1, i)

    def next_idx(i):
        return jnp.where(jnp.logical_and(lat(i), i % nb_seq < nb_seq - 1), i + 1, i)

    def ctx_idx(i):
        b = jnp.where(lat(i), i // nb_seq, (i - n_lat_tiles) // ctx_tiles)
        return t // n_ctx + b

    assert t % n_ctx == 0
    kern = functools.partial(_attn_kernel, nb_seq=nb_seq, n_lat_tiles=n_lat_tiles, n_kv=n_kv, group=group)
    kv_spec = lambda f, cb: pl.BlockSpec((TQ, kv_width), lambda i: (f(i), cb))
    return pl.pallas_call(
        kern,
        grid=(n_rows // TQ,),
        in_specs=[pl.BlockSpec(memory_space=pltpu.SMEM),
                  pl.BlockSpec((TQ, q_width), lambda i: (i, qcb)),
                  kv_spec(prev_idx, kcb), kv_spec(lambda i: i, kcb), kv_spec(next_idx, kcb),
                  kv_spec(prev_idx, vcb), kv_spec(lambda i: i, vcb), kv_spec(next_idx, vcb),
                  pl.BlockSpec((n_ctx, kv_width), lambda i: (ctx_idx(i), kcb)),
                  pl.BlockSpec((n_ctx, kv_width), lambda i: (ctx_idx(i), vcb))],
        out_specs=pl.BlockSpec((TQ, q_width), lambda i: (i, 0)),
        out_shape=jax.ShapeDtypeStruct((n_rows, q_width), BF16),
        compiler_params=_cparams("parallel"),
        name="window_attention",
    )(sink, proj, proj, proj, proj, proj, proj, proj, proj, proj)


def _merge_kernel(*refs, two_src, tiles_per_seq, n_lat_tiles, alpha):
    if two_src:
        xa_ref, xb_ref = refs[:2]
        refs = refs[2:]
    else:
        xa_ref = refs[0]
        refs = refs[1:]
    (attn_ref, cb_ref, cc_ref, cu_ref, ga_ref, gc_ref, ccp_ref, cup_ref, ccn_ref, cun_ref,
     mod_ref, cw_ref, wa_ref, wc_ref, wo_ref, g_ref, b_ref, o_ref) = refs
    i = pl.program_id(0)
    tm = xa_ref.shape[0]
    hr = ccp_ref.shape[0]
    p = i % tiles_per_seq
    is_lat = i < n_lat_tiles
    first = jnp.where(is_lat, p == 0, True)
    last = jnp.where(is_lat, p == tiles_per_seq - 1, True)
    x = jnp.where(is_lat, xa_ref[...], xb_ref[...]) if two_src else xa_ref[...]

    up = cc_ref[...].astype(F32) * cu_ref[...].astype(F32)
    prev_row = ccp_ref[hr - 1:hr, :].astype(F32) * cup_ref[hr - 1:hr, :].astype(F32)
    next_row = ccn_ref[0:1, :].astype(F32) * cun_ref[0:1, :].astype(F32)
    prev_row = jnp.where(first, 0.0, prev_row)
    next_row = jnp.where(last, 0.0, next_row)
    ridx = lax.broadcasted_iota(jnp.int32, up.shape, 0)
    dn = jnp.where(ridx == 0, prev_row, pltpu.roll(up, 1, 0))
    un = jnp.where(ridx == tm - 1, next_row, pltpu.roll(up, tm - 1, 0))
    conv = cw_ref[0:1, :] * dn + cw_ref[1:2, :] * up + cw_ref[2:3, :] * un
    sconv = (cb_ref[...].astype(F32) * conv).astype(BF16)

    a = jnp.dot(attn_ref[...], wa_ref[...], preferred_element_type=F32)
    s = jnp.dot(sconv, wc_ref[...], preferred_element_type=F32)
    m = (ga_ref[...].astype(F32) * a + gc_ref[...].astype(F32) * s).astype(BF16)
    y = jnp.dot(m, wo_ref[...], preferred_element_type=F32)
    z = alpha * x + mod_ref[0, 2:3, :] * y
    o_ref[...] = _layer_norm(z, g_ref[...], b_ref[...])


def _merge(x_lat, x_ctx, attn, proj, mod, conv_w, wa, wc, wo, g, b, *, n_rows, seg_of_row_tile, d, n_seq, n_ctx,
           n_batch, alpha):
    hr = 16
    assert n_seq % TM == 0 and n_ctx == TM
    tiles_per_seq = n_seq // TM
    n_lat_tiles = n_batch * tiles_per_seq
    two_src = x_ctx is not None
    kern = functools.partial(_merge_kernel, two_src=two_src, tiles_per_seq=tiles_per_seq, n_lat_tiles=n_lat_tiles,
                             alpha=alpha)
    col = lambda cb: pl.BlockSpec((TM, d), lambda i: (i, cb))
    rpt = TM // hr
    n_hblk = proj.shape[0] // hr
    prev = lambda cb: pl.BlockSpec((hr, d), lambda i: (jnp.maximum(i * rpt - 1, 0), cb))
    nxt = lambda cb: pl.BlockSpec((hr, d), lambda i: (jnp.minimum((i + 1) * rpt, n_hblk - 1), cb))
    const = lambda shape: pl.BlockSpec(shape, lambda i: (0,) * len(shape), pipeline_mode=pl.Buffered(1))
    if two_src:
        x_specs = [pl.BlockSpec((TM, d), lambda i: (jnp.minimum(i, n_lat_tiles - 1), 0)),
                   pl.BlockSpec((TM, d), lambda i: (jnp.maximum(i - n_lat_tiles, 0), 0))]
        x_args = (x_lat, x_ctx)
    else:
        x_specs = [pl.BlockSpec((TM, d), lambda i: (i, 0))]
        x_args = (x_lat,)
    return pl.pallas_call(
        kern,
        grid=(n_rows // TM,),
        in_specs=x_specs + [pl.BlockSpec((TM, attn.shape[1]), lambda i: (i, 0)),
                            col(0), col(1), col(2), col(3), col(4),
                            prev(1), prev(2), nxt(1), nxt(2),
                            pl.BlockSpec((1, 6, d), lambda i: (seg_of_row_tile(i, TM), 0, 0)),
                            const((3, d)), const(wa.shape), const(wc.shape), const(wo.shape),
                            const((1, d)), const((1, d))],
        out_specs=pl.BlockSpec((TM, d), lambda i: (i, 0)),
        out_shape=jax.ShapeDtypeStruct((n_rows, d), F32),
        compiler_params=_cparams("parallel"),
        name="merge_ln1",
    )(*x_args, attn, proj, proj, proj, proj, proj, proj, proj, proj, proj, mod, conv_w, wa, wc, wo,
      g.reshape(1, d), b.reshape(1, d))


def _router_kernel(x_ref, mod_ref, rwt_ref, rb_ref, e_ref, w_ref, rank_ref, cnt_ref, cnt_scr, *, n_groups):
    i = pl.program_id(0)
    tm = x_ref.shape[0]
    n_exp = rwt_ref.shape[0]
    per = n_exp // n_groups

    @pl.when(i == 0)
    def _():
        cnt_scr[...] = jnp.zeros_like(cnt_scr)

    h2 = x_ref[...] * (1.0 + mod_ref[0, 4:5, :]) + mod_ref[0, 3:4, :]
    logits = lax.dot_general(rwt_ref[...].astype(BF16), h2.astype(BF16), NT_DIMS,
                             preferred_element_type=F32)
    ex = jnp.exp(logits - jnp.max(logits, axis=0, keepdims=True))
    probs = ex / jnp.sum(ex, axis=0, keepdims=True)
    sel3 = (probs + rb_ref[...]).reshape(n_groups, per, tm)

    li = lax.broadcasted_iota(jnp.int32, (n_groups, per, tm), 1).astype(F32)
    m1 = jnp.max(sel3, axis=1, keepdims=True)
    i1 = jnp.min(jnp.where(sel3 == m1, li, float(per)), axis=1, keepdims=True)
    sel3b = jnp.where(li == i1, -jnp.inf, sel3)
    m2 = jnp.max(sel3b, axis=1, keepdims=True)
    i2 = jnp.min(jnp.where(sel3b == m2, li, float(per)), axis=1, keepdims=True)
    score = m1 + m2
    gi = lax.broadcasted_iota(jnp.int32, (n_groups, 1, tm), 0).astype(F32)
    gbest = jnp.min(jnp.where(score == jnp.max(score, axis=0, keepdims=True), gi, float(n_groups)),
                    axis=0, keepdims=True)
    pick = gi == gbest
    l1 = jnp.sum(jnp.where(pick, i1, 0.0), axis=0)
    l2 = jnp.sum(jnp.where(pick, i2, 0.0), axis=0)
    e1 = gbest[0] * per + l1
    e2 = gbest[0] * per + l2

    ei = lax.broadcasted_iota(jnp.int32, (n_exp, tm), 0).astype(F32)
    is1 = ei == e1
    is2 = ei == e2
    p1 = jnp.sum(jnp.where(is1, probs, 0.0), axis=0, keepdims=True)
    p2 = jnp.sum(jnp.where(is2, probs, 0.0), axis=0, keepdims=True)
    psum = p1 + p2

    onehot = jnp.where(jnp.logical_or(is1, is2), 1.0, 0.0)
    srow = lax.broadcasted_iota(jnp.int32, (tm, tm), 0)
    scol = lax.broadcasted_iota(jnp.int32, (tm, tm), 1)
    before = jnp.where(srow < scol, 1.0, 0.0).astype(BF16)
    excl = jnp.dot(onehot.astype(BF16), before, preferred_element_type=F32) + cnt_scr[:, 0:1]
    r1 = jnp.sum(jnp.where(is1, excl, 0.0), axis=0, keepdims=True)
    r2 = jnp.sum(jnp.where(is2, excl, 0.0), axis=0, keepdims=True)
    cnt_scr[...] = cnt_scr[...] + jnp.sum(onehot, axis=1, keepdims=True)

    e_ref[0:1, :] = e1.astype(jnp.int32)
    e_ref[1:2, :] = e2.astype(jnp.int32)
    w_ref[0:1, :] = p1 / psum
    w_ref[1:2, :] = p2 / psum
    rank_ref[0:1, :] = r1.astype(jnp.int32)
    rank_ref[1:2, :] = r2.astype(jnp.int32)
    cnt_ref[...] = cnt_scr[...]


def _router(x1, mod, rwt, rb, *, seg_of_row_tile, d):
    r = x1.shape[0]
    n_exp = rwt.shape[0]
    row2 = lambda dt: jax.ShapeDtypeStruct((TOP_K, r), dt)
    spec2 = pl.BlockSpec((TOP_K, TM_ROUTER), lambda i: (0, i))
    assert r % TM_ROUTER == 0
    return pl.pallas_call(
        functools.partial(_router_kernel, n_groups=N_EXPERT_GROUPS),
        grid=(r // TM_ROUTER,),
        in_specs=[pl.BlockSpec((TM_ROUTER, d), lambda i: (i, 0)),
                  pl.BlockSpec((1, 6, d), lambda i: (seg_of_row_tile(i, TM_ROUTER), 0, 0)),
                  pl.BlockSpec((n_exp, d), lambda i: (0, 0)),
                  pl.BlockSpec((n_exp, 1), lambda i: (0, 0))],
        out_specs=[spec2, spec2, spec2, pl.BlockSpec((n_exp, LANE), lambda i: (0, 0))],
        out_shape=[row2(jnp.int32), row2(F32), row2(jnp.int32), jax.ShapeDtypeStruct((n_exp, LANE), F32)],
        scratch_shapes=[pltpu.VMEM((n_exp, LANE), F32)],
        compiler_params=_cparams("arbitrary"),
        name="router",
    )(x1, mod, rwt, rb.reshape(n_exp, 1))


def _dispatch_kernel(dest_ref, fill_lo_ref, fill_hi_ref, x_ref, mod_ref, xs_ref, buf, zslab, sem, zsem, *, n_rows):
    i = pl.program_id(0)
    n_steps = pl.num_programs(0)
    tm = x_ref.shape[0]
    words = _pack_rows(x_ref[...] * (1.0 + mod_ref[0, 4:5, :]) + mod_ref[0, 3:4, :])

    def drain():
        for _ in range(TOP_K):
            pltpu.make_async_copy(buf, xs_ref.at[pl.ds(0, tm * SUBLANE)], sem).wait()

    @pl.when(i > 0)
    def _():
        drain()

    for s, w in enumerate(words):
        buf[pl.ds(s, tm, stride=SUBLANE), :] = w

    def issue(t, carry):
        src = buf.at[pl.ds(pl.multiple_of(t * SUBLANE, SUBLANE), SUBLANE)]
        for k in range(TOP_K):
            dst = dest_ref[k * n_rows + i * tm + t]
            pltpu.make_async_copy(src, xs_ref.at[pl.ds(pl.multiple_of(dst * SUBLANE, SUBLANE), SUBLANE)],
                                  sem).start(priority=k % 2)
        return carry

    lax.fori_loop(0, tm, issue, 0, unroll=ISSUE_UNROLL)

    @pl.when(i == n_steps - 1)
    def _():
        drain()
        zslab[...] = jnp.zeros_like(zslab)
        n_ranges = fill_lo_ref.shape[0]

        def zero_copy(rr):
            return pltpu.make_async_copy(zslab, xs_ref.at[pl.ds(pl.multiple_of(rr * SUBLANE, SUBLANE), SUBLANE)], zsem)

        def per_range(fn):
            def body(e, carry):
                lax.fori_loop(fill_lo_ref[e], fill_hi_ref[e], lambda rr, c: (fn(rr), c)[1], 0)
                return carry
            lax.fori_loop(0, n_ranges, body, 0)

        per_range(lambda rr: zero_copy(rr).start())
        per_range(lambda rr: zero_copy(rr).wait())


def _dispatch(x1, mod, dest_flat, fill_lo, fill_hi, cap, *, seg_of_row_tile, d):
    r = x1.shape[0]
    slab = _slab_rows(d)
    grid_spec = pltpu.PrefetchScalarGridSpec(
        num_scalar_prefetch=3,
        grid=(r // TM,),
        in_specs=[pl.BlockSpec((TM, d), lambda i, *_: (i, 0)),
                  pl.BlockSpec((1, 6, d), lambda i, *_: (seg_of_row_tile(i, TM), 0, 0))],
        out_specs=pl.BlockSpec(memory_space=pl.ANY),
        scratch_shapes=[pltpu.VMEM((TM * slab, LANE), U32), pltpu.VMEM((slab, LANE), U32),
                        pltpu.SemaphoreType.DMA(()), pltpu.SemaphoreType.DMA(())],
    )
    return pl.pallas_call(
        functools.partial(_dispatch_kernel, n_rows=r),
        grid_spec=grid_spec,
        out_shape=jax.ShapeDtypeStruct((cap * slab, LANE), U32),
        compiler_params=_cparams("arbitrary"),
        name="moe_dispatch",
    )(dest_flat, fill_lo, fill_hi, x1, mod)


def _expert_kernel(blk_e_ref, n_used_ref, xs_ref, wg_ref, wu_ref, wd_ref, o_ref):
    b = pl.program_id(0)
    mb = xs_ref.shape[0] // SUBLANE

    @pl.when(b < n_used_ref[0])
    def _():
        xb = _unpack_rows([xs_ref[pl.ds(s, mb, stride=SUBLANE), :] for s in range(SUBLANE)]).astype(BF16)
        g = jnp.dot(xb, wg_ref[...], preferred_element_type=F32)
        u = jnp.dot(xb, wu_ref[...], preferred_element_type=F32)
        hmid = (g * jax.nn.sigmoid(g) * u).astype(BF16)
        y = jnp.dot(hmid, wd_ref[...], preferred_element_type=F32)
        for s, w in enumerate(_pack_rows(y)):
            o_ref[pl.ds(s, mb, stride=SUBLANE), :] = w

    @pl.when(b >= n_used_ref[0])
    def _():
        o_ref[...] = jnp.zeros_like(o_ref)


def _experts(xs, blk_expert, n_used, wg, wu, wd):
    d, ff = wg.shape[1], wg.shape[2]
    slab = _slab_rows(d)
    n_blk = xs.shape[0] // (MOE_BLOCK * slab)
    xrow = lambda b, be, nu: (jnp.minimum(b, nu[0] - 1), 0)
    grid_spec = pltpu.PrefetchScalarGridSpec(
        num_scalar_prefetch=2,
        grid=(n_blk,),
        in_specs=[pl.BlockSpec((MOE_BLOCK * slab, LANE), xrow),
                  pl.BlockSpec((None, d, ff), lambda b, be, nu: (be[b], 0, 0)),
                  pl.BlockSpec((None, d, ff), lambda b, be, nu: (be[b], 0, 0)),
                  pl.BlockSpec((None, ff, d), lambda b, be, nu: (be[b], 0, 0))],
        out_specs=pl.BlockSpec((MOE_BLOCK * slab, LANE), lambda b, be, nu: (b, 0)),
    )
    return pl.pallas_call(
        _expert_kernel,
        grid_spec=grid_spec,
        out_shape=jax.ShapeDtypeStruct(xs.shape, U32),
        compiler_params=_cparams("arbitrary"),
        name="moe_experts",
    )(blk_expert, n_used, xs, wg, wu, wd)


def _combine_kernel(*refs, n_rows, alpha, emit_h):
    if emit_h:
        dest_ref, x_ref, wt_ref, mod_ref, g_ref, b_ref, modn_ref, ys_ref, o_ref, h_ref, buf, sems = refs
    else:
        dest_ref, x_ref, wt_ref, mod_ref, g_ref, b_ref, ys_ref, o_ref, buf, sems = refs
    i = pl.program_id(0)
    n_steps = pl.num_programs(0)
    tm = x_ref.shape[0]
    slot = i % 2

    def issue_tile(tile, sl):
        def body(t, carry):
            for k in range(TOP_K):
                src = dest_ref[k * n_rows + tile * tm + t]
                pltpu.make_async_copy(ys_ref.at[pl.ds(pl.multiple_of(src * SUBLANE, SUBLANE), SUBLANE)],
                                      buf.at[sl, k, pl.ds(pl.multiple_of(t * SUBLANE, SUBLANE), SUBLANE)],
                                      sems.at[sl]).start(priority=k % 2)
            return carry
        lax.fori_loop(0, tm, body, 0, unroll=ISSUE_UNROLL)

    @pl.when(i == 0)
    def _():
        issue_tile(0, 0)

    @pl.when(i + 1 < n_steps)
    def _():
        issue_tile(i + 1, 1 - slot)

    def finish(sl):
        for k in range(TOP_K):
            pltpu.make_async_copy(ys_ref.at[pl.ds(0, tm * SUBLANE)], buf.at[sl, k], sems.at[sl]).wait()
        y = [_unpack_rows([buf[sl, k, pl.ds(s, tm, stride=SUBLANE), :] for s in range(SUBLANE)])
             for k in range(TOP_K)]
        f = wt_ref[:, 0:1] * y[0] + wt_ref[:, 1:2] * y[1]
        z = alpha * x_ref[...] + mod_ref[0, 5:6, :] * f
        x2 = _layer_norm(z, g_ref[...], b_ref[...])
        o_ref[...] = x2
        if emit_h:
            h_ref[...] = (x2 * (1.0 + modn_ref[0, 1:2, :]) + modn_ref[0, 0:1, :]).astype(BF16)

    for sl in range(2):
        pl.when(slot == sl)(functools.partial(finish, sl))


def _combine(x1, wt_rows, mod, g, b, dest_flat, ys, mod_next, *, seg_of_row_tile, d, alpha):
    r = x1.shape[0]
    slab = _slab_rows(d)
    emit_h = mod_next is not None
    mod_spec = pl.BlockSpec((1, 6, d), lambda i, *_: (seg_of_row_tile(i, TM), 0, 0))
    row_spec = pl.BlockSpec((TM, d), lambda i, *_: (i, 0))
    vec_spec = pl.BlockSpec((1, d), lambda i, *_: (0, 0))
    in_specs = [row_spec, pl.BlockSpec((TM, TOP_K), lambda i, *_: (i, 0)), mod_spec, vec_spec, vec_spec]
    args = [x1, wt_rows, mod, g.reshape(1, d), b.reshape(1, d)]
    out_specs, out_shape = [row_spec], [jax.ShapeDtypeStruct((r, d), F32)]
    if emit_h:
        in_specs.append(mod_spec)
        args.append(mod_next)
        out_specs.append(row_spec)
        out_shape.append(jax.ShapeDtypeStruct((r, d), BF16))
    in_specs.append(pl.BlockSpec(memory_space=pl.ANY))
    args.append(ys)
    grid_spec = pltpu.PrefetchScalarGridSpec(
        num_scalar_prefetch=1,
        grid=(r // TM,),
        in_specs=in_specs,
        out_specs=out_specs,
        scratch_shapes=[pltpu.VMEM((2, TOP_K, TM * slab, LANE), U32), pltpu.SemaphoreType.DMA((2,))],
    )
    outs = pl.pallas_call(
        functools.partial(_combine_kernel, n_rows=r, alpha=alpha, emit_h=emit_h),
        grid_spec=grid_spec,
        out_shape=out_shape,
        compiler_params=_cparams("arbitrary"),
        name="moe_combine_ln2",
    )(dest_flat, *args)
    return (outs[0], outs[1]) if emit_h else (outs[0], None)


def _rope_tables(n_batch, n_seq, ctx_rows):
    nf = HEAD_DIM // 4
    inv_freq = np.power(np.float32(ROPE_BASE), -np.arange(nf, dtype=np.float32) / np.float32(nf)).astype(np.float32)
    n_grid_rows = n_seq // GRID_W
    ar = (np.arange(n_grid_rows, dtype=np.float32)[:, None] * inv_freq).astype(np.float32)
    ac = (np.arange(GRID_W, dtype=np.float32)[:, None] * inv_freq).astype(np.float32)
    shape = (n_grid_rows, GRID_W, nf)
    by_row = lambda a: jnp.broadcast_to(jnp.asarray(a, F32)[:, None, :], shape)
    by_col = lambda a: jnp.broadcast_to(jnp.asarray(a, F32)[None, :, :], shape)
    cr, sr, cc, sc = np.cos(ar), np.sin(ar), np.cos(ac), np.sin(ac)
    cos_t = jnp.concatenate([by_row(cr), by_row(cr), by_col(cc), by_col(cc)], axis=-1).reshape(n_seq, HEAD_DIM)
    sin_t = jnp.concatenate([by_row(-sr), by_row(sr), by_col(-sc), by_col(sc)], axis=-1).reshape(n_seq, HEAD_DIM)
    cos_t = jnp.concatenate([cos_t] * n_batch + [jnp.ones((ctx_rows, HEAD_DIM), F32)], axis=0)
    sin_t = jnp.concatenate([sin_t] * n_batch + [jnp.zeros((ctx_rows, HEAD_DIM), F32)], axis=0)
    return cos_t, sin_t


def kernel(x, c, ctx, c_ctx, w_ada, b_ada, w_in, attn_sink, conv_w, w_attn_proj, w_conv_proj, w_out,
           ln1_g, ln1_b, ln2_g, ln2_b, router_w, router_b, w_gate, w_up, w_down):
    n_batch, n_seq, d = x.shape
    n_ctx = ctx.shape[1]
    depth = w_ada.shape[0]
    n_exp = router_w.shape[1]
    q_width = attn_sink.shape[1] * HEAD_DIM
    in_width = w_in.shape[2]
    kv_width = (in_width - q_width - 5 * d) // 2
    t = n_batch * n_seq
    r = t + n_batch * n_ctx
    alpha = float((2 * depth) ** 0.25)
    assert w_conv_proj.shape[1] == d and n_seq % GRID_W == 0
    assert r % TM_IN == 0 and r % TM == 0 and r % TQ == 0

    def seg_of_row_tile(i, tile):
        return jnp.minimum((i * tile) // n_seq, n_batch)

    x_lat, x_ctx = x.reshape(t, d), ctx.reshape(n_batch * n_ctx, d)
    rope_c, rope_s = _rope_tables(n_batch, n_seq, n_batch * n_ctx)
    c_rows = jnp.concatenate([c, c_ctx[None, :], jnp.zeros((8 - n_batch - 1, d), F32)], axis=0)
    mods = [_mod_table(c_rows, w_ada, b_ada[l], l).reshape(8, 6, d) for l in range(depth)]
    rwt = router_w.T
    q_col, k_col, v_col = 5 * d, 5 * d + q_width, 5 * d + q_width + kv_width
    expert_ids = jnp.arange(n_exp, dtype=jnp.int32)

    h = _modulate(x_lat, x_ctx, mods[0], seg_of_row_tile=seg_of_row_tile)
    x_cur = None
    for l in range(depth):
        last = l == depth - 1
        mod = mods[l]
        n_rows = t if last else r
        proj, wg_b, wu_b, wd_b = _in_proj(h, w_in, l, rope_c, rope_s, w_gate, w_up, w_down, d=d, q_width=q_width,
                                          kv_width=kv_width)
        attn = _attention(proj, attn_sink[l], n_rows=n_rows, n_batch=n_batch, n_seq=n_seq, n_ctx=n_ctx, q_col=q_col,
                          k_col=k_col, v_col=v_col, q_width=q_width, kv_width=kv_width)
        src = (x_lat, x_ctx) if l == 0 else (x_cur, None)
        x1 = _merge(src[0], src[1], attn, proj, mod, conv_w[l], w_attn_proj[l].astype(BF16),
                    w_conv_proj[l].astype(BF16), w_out[l].astype(BF16), ln1_g[l], ln1_b[l], n_rows=n_rows,
                    seg_of_row_tile=seg_of_row_tile, d=d, n_seq=n_seq, n_ctx=n_ctx, n_batch=n_batch, alpha=alpha)

        e_idx, wts, rank, cnt = _router(x1, mod, rwt, router_b, seg_of_row_tile=seg_of_row_tile, d=d)
        cap = (n_rows * TOP_K + n_exp * (MOE_BLOCK - 1) + MOE_BLOCK - 1) // MOE_BLOCK * MOE_BLOCK
        n_blk = cap // MOE_BLOCK
        counts = cnt[:, 0].astype(jnp.int32)
        padded = (counts + MOE_BLOCK - 1) // MOE_BLOCK * MOE_BLOCK
        pend = jnp.cumsum(padded)
        pstart = pend - padded
        slot0 = jnp.sum(jnp.where(e_idx[:, :, None] == expert_ids, pstart, 0), axis=-1)
        dest_flat = (slot0 + rank).reshape(-1)
        n_used = (pend[-1] // MOE_BLOCK).astype(jnp.int32)
        blk_start = jnp.minimum(jnp.arange(n_blk, dtype=jnp.int32), n_used - 1) * MOE_BLOCK
        blk_expert = jnp.sum(blk_start[:, None] >= pend[None, :], axis=-1).astype(jnp.int32)
        fill_lo = jnp.concatenate([pstart + counts, pend[-1:]]).astype(jnp.int32)
        fill_hi = jnp.concatenate([pend, jnp.full((1,), cap, pend.dtype)]).astype(jnp.int32)
        xs = _dispatch(x1, mod, dest_flat, fill_lo, fill_hi, cap, seg_of_row_tile=seg_of_row_tile, d=d)
        ys = _experts(xs, blk_expert, n_used.reshape(1), wg_b, wu_b, wd_b)
        x_cur, h = _combine(x1, wts.T, mod, ln2_g[l], ln2_b[l], dest_flat, ys, None if last else mods[l + 1],
                            seg_of_row_tile=seg_of_row_tile, d=d, alpha=alpha)

    return x_cur.reshape(n_batch, n_seq, d)
```

```python
import functools

import numpy as np
import jax
import jax.numpy as jnp
from jax import lax
from jax.experimental import pallas as pl
from jax.experimental.pallas import tpu as pltpu

F32 = jnp.float32
BF16 = jnp.bfloat16
U32 = jnp.uint32
HIGHEST = lax.Precision.HIGHEST

GRID_W = 64
HEAD_DIM = 128
WINDOW = 128
ROPE_BASE = 10000.0
N_EXPERT_GROUPS = 4
TOP_K = 2
LN_EPS = 1e-5
NEG_INF = -1e30

LANE = 128
SUBLANE = 8
TQ = WINDOW
TM = 256
TM_IN = 1040
TN_IN = 1024
TN_CHUNK = 256
EXPERT_CAST_BLOCKS = 256
TM_ROUTER = 512
TN_MOD = 1024
MOE_BLOCK = 256
ISSUE_UNROLL = 8
COMBINE_SLOTS = 3
VMEM_LIMIT = 56 * 1024 * 1024

NT_DIMS = (((1,), (1,)), ((), ()))
TN_DIMS = (((0,), (0,)), ((), ()))
LOG2E = 1.4426950408889634


def _cparams(*sem):
    return pltpu.CompilerParams(dimension_semantics=sem, vmem_limit_bytes=VMEM_LIMIT)


def _layer_norm(z, g, b):
    mu = jnp.mean(z, axis=-1, keepdims=True)
    zc = z - mu
    var = jnp.mean(zc * zc, axis=-1, keepdims=True)
    return zc * lax.rsqrt(var + LN_EPS) * g + b


def _pack_rows(v):
    half = v.shape[1] // 2
    bits = lax.bitcast_convert_type(v.astype(BF16).astype(F32), U32)
    words = []
    for s in range(half // LANE):
        lo = bits[:, s * LANE:(s + 1) * LANE] >> 16
        hi = bits[:, half + s * LANE:half + (s + 1) * LANE] & np.uint32(0xFFFF0000)
        words.append(lo | hi)
    return words


def _unpack_rows(words):
    lo = [lax.bitcast_convert_type(w << 16, F32) for w in words]
    hi = [lax.bitcast_convert_type(w & np.uint32(0xFFFF0000), F32) for w in words]
    return jnp.concatenate(lo + hi, axis=1)


def _slab_rows(d):
    assert d % (2 * LANE) == 0 and d // (2 * LANE) == SUBLANE, "one row must pack into one (8, 128) uint32 tile"
    return d // (2 * LANE)


def _mod_kernel(c_ref, w_ref, b_ref, o_ref):
    cc = c_ref[...]
    s = cc * jax.nn.sigmoid(cc)
    o_ref[...] = jnp.dot(s, w_ref[...], preferred_element_type=F32, precision=HIGHEST) + b_ref[...]


def _mod_table(c_rows, w_ada, b_ada_l, layer):
    rows, d = c_rows.shape
    n_out = w_ada.shape[2]
    return pl.pallas_call(
        _mod_kernel,
        grid=(n_out // TN_MOD,),
        in_specs=[pl.BlockSpec((rows, d), lambda j: (0, 0)),
                  pl.BlockSpec((None, d, TN_MOD), lambda j: (layer, 0, j)),
                  pl.BlockSpec((1, TN_MOD), lambda j: (0, j))],
        out_specs=pl.BlockSpec((rows, TN_MOD), lambda j: (0, j)),
        out_shape=jax.ShapeDtypeStruct((rows, n_out), F32),
        compiler_params=_cparams("parallel"),
        name="mod_table",
    )(c_rows, w_ada, b_ada_l.reshape(1, n_out))


def _modulate_kernel(xa_ref, xb_ref, mod_ref, o_ref, *, n_lat_tiles):
    x = jnp.where(pl.program_id(0) < n_lat_tiles, xa_ref[...], xb_ref[...])
    o_ref[...] = (x * (1.0 + mod_ref[0, 1:2, :]) + mod_ref[0, 0:1, :]).astype(BF16)


def _modulate(x_lat, x_ctx, mod, *, seg_of_row_tile):
    t, d = x_lat.shape
    r = t + x_ctx.shape[0]
    n_lat = t // TM
    return pl.pallas_call(
        functools.partial(_modulate_kernel, n_lat_tiles=n_lat),
        grid=(r // TM,),
        in_specs=[pl.BlockSpec((TM, d), lambda i: (jnp.minimum(i, n_lat - 1), 0)),
                  pl.BlockSpec((TM, d), lambda i: (jnp.maximum(i - n_lat, 0), 0)),
                  pl.BlockSpec((1, 6, d), lambda i: (seg_of_row_tile(i, TM), 0, 0))],
        out_specs=pl.BlockSpec((TM, d), lambda i: (i, 0)),
        out_shape=jax.ShapeDtypeStruct((r, d), BF16),
        compiler_params=_cparams("parallel"),
        name="modulate_in",
    )(x_lat, x_ctx, mod)


def _inproj_kernel(h_ref, w_ref, rc_ref, rs_ref, eg_ref, eu_ref, ed_ref, o_ref, ego_ref, euo_ref, edo_ref, wb_scr, *,
                   n_plain, n_gate, n_q, k_cols, scale):
    j = pl.program_id(0)
    tn = o_ref.shape[1]

    @pl.when(pl.program_id(1) == 0)
    def _():
        wb_scr[...] = w_ref[...].astype(BF16)

    def rope(xh):
        lane = lax.broadcasted_iota(jnp.int32, xh.shape, 1)
        sw = jnp.where((lane & 32) == 0, pltpu.roll(xh, LANE - 32, 1), pltpu.roll(xh, 32, 1))
        return xh * rc_ref[...] + sw * rs_ref[...]

    def column_chunks(epilogue):
        ego_ref[...] = eg_ref[...].astype(BF16)
        euo_ref[...] = eu_ref[...].astype(BF16)
        edo_ref[...] = ed_ref[...].astype(BF16)
        for c0 in range(0, tn, TN_CHUNK):
            acc = jnp.dot(h_ref[...], wb_scr[:, c0:c0 + TN_CHUNK], preferred_element_type=F32)
            for c in range(c0, c0 + TN_CHUNK, LANE):
                o_ref[:, c:c + LANE] = epilogue(acc[:, c - c0:c - c0 + LANE], c).astype(BF16)

    @pl.when(j < n_plain)
    def _():
        column_chunks(lambda a, c: a)

    @pl.when((j >= n_plain) & (j < n_plain + n_gate))
    def _():
        column_chunks(lambda a, c: jax.nn.sigmoid(a))

    @pl.when((j >= n_plain + n_gate) & (j < n_plain + n_gate + n_q))
    def _():
        column_chunks(lambda a, c: rope(a) * scale)

    @pl.when(j == n_plain + n_gate + n_q)
    def _():
        column_chunks(lambda a, c: rope(a) if c < k_cols else a)


def _in_proj(h, w_in, layer, rope_c, rope_s, w_gate, w_up, w_down, *, d, q_width, kv_width):
    r = h.shape[0]
    in_width = w_in.shape[2]
    base = q_width + 2 * kv_width
    assert (3 * d) % TN_IN == 0 and (2 * d) % TN_IN == 0 and q_width % TN_IN == 0 and 2 * kv_width == TN_IN
    n_plain, n_gate, n_q = 3 * d // TN_IN, 2 * d // TN_IN, q_width // TN_IN
    n_j = in_width // TN_IN
    assert n_plain + n_gate + n_q + 1 == n_j and base % TN_IN == 0 and r % TM_IN == 0
    shift = base // TN_IN

    kern = functools.partial(_inproj_kernel, n_plain=n_plain, n_gate=n_gate, n_q=n_q, k_cols=kv_width,
                             scale=HEAD_DIM ** -0.5 * LOG2E)
    n_i = r // TM_IN
    depth, n_exp, _, ff = w_gate.shape
    assert n_j * n_i >= EXPERT_CAST_BLOCKS
    rows_gu, rows_d = n_exp * d // EXPERT_CAST_BLOCKS, n_exp * ff // EXPERT_CAST_BLOCKS
    assert rows_gu * EXPERT_CAST_BLOCKS == n_exp * d and rows_d * EXPERT_CAST_BLOCKS == n_exp * ff
    assert rows_gu % 16 == 0 and rows_d % 16 == 0

    def cast_blk(j, i):
        return jnp.minimum(j * n_i + i, EXPERT_CAST_BLOCKS - 1)

    cast_in = lambda rows, cols: pl.BlockSpec((None, rows, cols), lambda j, i: (layer, cast_blk(j, i), 0))
    cast_out = lambda rows, cols: pl.BlockSpec((rows, cols), lambda j, i: (cast_blk(j, i), 0))
    proj, wg_b, wu_b, wd_b = pl.pallas_call(
        kern,
        grid=(n_j, n_i),
        in_specs=[pl.BlockSpec((TM_IN, d), lambda j, i: (i, 0)),
                  pl.BlockSpec((None, d, TN_IN), lambda j, i: (layer, 0, (j + shift) % n_j)),
                  pl.BlockSpec((TM_IN, LANE), lambda j, i: (i, 0)),
                  pl.BlockSpec((TM_IN, LANE), lambda j, i: (i, 0)),
                  cast_in(rows_gu, ff), cast_in(rows_gu, ff), cast_in(rows_d, d)],
        out_specs=[pl.BlockSpec((TM_IN, TN_IN), lambda j, i: (i, j)),
                   cast_out(rows_gu, ff), cast_out(rows_gu, ff), cast_out(rows_d, d)],
        out_shape=[jax.ShapeDtypeStruct((r, in_width), BF16),
                   jax.ShapeDtypeStruct((n_exp * d, ff), BF16), jax.ShapeDtypeStruct((n_exp * d, ff), BF16),
                   jax.ShapeDtypeStruct((n_exp * ff, d), BF16)],
        scratch_shapes=[pltpu.VMEM((d, TN_IN), BF16)],
        compiler_params=_cparams("arbitrary", "arbitrary"),
        name="in_proj",
    )(h, w_in, rope_c, rope_s, w_gate.reshape(depth, n_exp * d, ff), w_up.reshape(depth, n_exp * d, ff),
      w_down.reshape(depth, n_exp * ff, d))
    return proj, wg_b.reshape(n_exp, d, ff), wu_b.reshape(n_exp, d, ff), wd_b.reshape(n_exp, ff, d)


def _attn_kernel(sink_ref, q_ref, kp_ref, kc_ref, kn_ref, vp_ref, vc_ref, vn_ref, kx_ref, vx_ref, o_ref, *,
                 nb_seq, n_lat_tiles, n_kv, group):
    i = pl.program_id(0)
    tq = q_ref.shape[0]
    is_lat = i < n_lat_tiles
    p = i % nb_seq
    has_prev = jnp.logical_and(is_lat, p > 0)
    has_next = jnp.logical_and(is_lat, p < nb_seq - 1)
    n_ctx = kx_ref.shape[0]
    key = lax.broadcasted_iota(jnp.int32, (tq, tq), 0)
    qry = lax.broadcasted_iota(jnp.int32, (tq, tq), 1)
    b_prev = jnp.where(jnp.logical_and(key >= qry, has_prev), 0.0, NEG_INF)
    b_cur = jnp.where(jnp.logical_and(key >= 0, is_lat), 0.0, NEG_INF)
    b_next = jnp.where(jnp.logical_and(key <= qry, has_next), 0.0, NEG_INF)
    bias_t = jnp.concatenate([b_prev, b_cur, b_next, jnp.zeros((n_ctx, tq), F32)], axis=0).astype(BF16)
    qrow = lax.broadcasted_iota(jnp.int32, (group * tq, tq), 0)
    qcol = lax.broadcasted_iota(jnp.int32, (group * tq, tq), 1)
    row_onehot = jnp.where(qrow % tq == qcol, 1.0, 0.0).astype(BF16)

    for h in range(n_kv):
        hs = slice(h * HEAD_DIM, (h + 1) * HEAD_DIM)
        q4 = jnp.concatenate([q_ref[:, (h * group + g) * HEAD_DIM:(h * group + g + 1) * HEAD_DIM]
                              for g in range(group)], axis=0)
        k_all = jnp.concatenate([kp_ref[:, hs], kc_ref[:, hs], kn_ref[:, hs], kx_ref[:, hs]], axis=0)
        v_all = jnp.concatenate([vp_ref[:, hs], vc_ref[:, hs], vn_ref[:, hs], vx_ref[:, hs]], axis=0)
        s_t = lax.dot_general(jnp.concatenate([k_all, bias_t], axis=1), jnp.concatenate([q4, row_onehot], axis=1),
                              NT_DIMS, preferred_element_type=F32)
        p_cols, inv_cols = [], []
        for g in range(group):
            sg = s_t[:, g * tq:(g + 1) * tq]
            sink2 = sink_ref[h * group + g] * LOG2E
            m = jnp.maximum(jnp.max(sg, axis=0, keepdims=True), sink2)
            p = jnp.exp2(sg - m)
            denom = jnp.sum(p, axis=0, keepdims=True) + jnp.exp2(sink2 - m)
            p_cols.append(p.astype(BF16))
            inv_cols.append(1.0 / denom)
        o_t = lax.dot_general(v_all, jnp.concatenate(p_cols, axis=1), TN_DIMS, preferred_element_type=F32)
        o_t = o_t * jnp.concatenate(inv_cols, axis=1)
        for g in range(group):
            hq = h * group + g
            o_ref[:, hq * HEAD_DIM:(hq + 1) * HEAD_DIM] = o_t[:, g * tq:(g + 1) * tq].T.astype(BF16)


def _attention(proj, sink, *, n_rows, n_batch, n_seq, n_ctx, q_col, k_col, v_col, q_width, kv_width):
    t = n_batch * n_seq
    nb_seq = n_seq // TQ
    n_lat_tiles = t // TQ
    n_kv = kv_width // HEAD_DIM
    group = q_width // kv_width
    assert n_ctx % TQ == 0 and q_col % q_width == 0 and k_col % kv_width == 0 and v_col % kv_width == 0
    ctx_tiles = n_ctx // TQ
    kcb, vcb, qcb = k_col // kv_width, v_col // kv_width, q_col // q_width

    def lat(i):
        return i < n_lat_tiles

    def prev_idx(i):
        return jnp.where(jnp.logical_and(lat(i), i % nb_seq > 0), i - 1, i)

    def next_idx(i):
        return jnp.where(jnp.logical_and(lat(i), i % nb_seq < nb_seq - 1), i + 1, i)

    def ctx_idx(i):
        b = jnp.where(lat(i), i // nb_seq, (i - n_lat_tiles) // ctx_tiles)
        return t // n_ctx + b

    assert t % n_ctx == 0
    kern = functools.partial(_attn_kernel, nb_seq=nb_seq, n_lat_tiles=n_lat_tiles, n_kv=n_kv, group=group)
    kv_spec = lambda f, cb: pl.BlockSpec((TQ, kv_width), lambda i: (f(i), cb))
    return pl.pallas_call(
        kern,
        grid=(n_rows // TQ,),
        in_specs=[pl.BlockSpec(memory_space=pltpu.SMEM),
                  pl.BlockSpec((TQ, q_width), lambda i: (i, qcb)),
                  kv_spec(prev_idx, kcb), kv_spec(lambda i: i, kcb), kv_spec(next_idx, kcb),
                  kv_spec(prev_idx, vcb), kv_spec(lambda i: i, vcb), kv_spec(next_idx, vcb),
                  pl.BlockSpec((n_ctx, kv_width), lambda i: (ctx_idx(i), kcb)),
                  pl.BlockSpec((n_ctx, kv_width), lambda i: (ctx_idx(i), vcb))],
        out_specs=pl.BlockSpec((TQ, q_width), lambda i: (i, 0)),
        out_shape=jax.ShapeDtypeStruct((n_rows, q_width), BF16),
        compiler_params=_cparams("parallel"),
        name="window_attention",
    )(sink, proj, proj, proj, proj, proj, proj, proj, proj, proj)


def _merge_kernel(*refs, two_src, tiles_per_seq, n_lat_tiles, alpha):
    if two_src:
        xa_ref, xb_ref = refs[:2]
        refs = refs[2:]
    else:
        xa_ref = refs[0]
        refs = refs[1:]
    (attn_ref, cb_ref, cc_ref, cu_ref, ga_ref, gc_ref, ccp_ref, cup_ref, ccn_ref, cun_ref,
     mod_ref, cw_ref, wa_ref, wc_ref, wo_ref, g_ref, b_ref, o_ref) = refs
    i = pl.program_id(0)
    tm = xa_ref.shape[0]
    hr = ccp_ref.shape[0]
    p = i % tiles_per_seq
    is_lat = i < n_lat_tiles
    first = jnp.where(is_lat, p == 0, True)
    last = jnp.where(is_lat, p == tiles_per_seq - 1, True)
    x = jnp.where(is_lat, xa_ref[...], xb_ref[...]) if two_src else xa_ref[...]

    up = cc_ref[...].astype(F32) * cu_ref[...].astype(F32)
    prev_row = ccp_ref[hr - 1:hr, :].astype(F32) * cup_ref[hr - 1:hr, :].astype(F32)
    next_row = ccn_ref[0:1, :].astype(F32) * cun_ref[0:1, :].astype(F32)
    prev_row = jnp.where(first, 0.0, prev_row)
    next_row = jnp.where(last, 0.0, next_row)
    ridx = lax.broadcasted_iota(jnp.int32, up.shape, 0)
    dn = jnp.where(ridx == 0, prev_row, pltpu.roll(up, 1, 0))
    un = jnp.where(ridx == tm - 1, next_row, pltpu.roll(up, tm - 1, 0))
    conv = cw_ref[0:1, :] * dn + cw_ref[1:2, :] * up + cw_ref[2:3, :] * un
    sconv = (cb_ref[...].astype(F32) * conv).astype(BF16)

    a = jnp.dot(attn_ref[...], wa_ref[...], preferred_element_type=F32)
    s = jnp.dot(sconv, wc_ref[...], preferred_element_type=F32)
    m = (ga_ref[...].astype(F32) * a + gc_ref[...].astype(F32) * s).astype(BF16)
    y = jnp.dot(m, wo_ref[...], preferred_element_type=F32)
    z = alpha * x + mod_ref[0, 2:3, :] * y
    o_ref[...] = _layer_norm(z, g_ref[...], b_ref[...])


def _merge(x_lat, x_ctx, attn, proj, mod, conv_w, wa, wc, wo, g, b, *, n_rows, seg_of_row_tile, d, n_seq, n_ctx,
           n_batch, alpha):
    hr = 16
    assert n_seq % TM == 0 and n_ctx == TM
    tiles_per_seq = n_seq // TM
    n_lat_tiles = n_batch * tiles_per_seq
    two_src = x_ctx is not None
    kern = functools.partial(_merge_kernel, two_src=two_src, tiles_per_seq=tiles_per_seq, n_lat_tiles=n_lat_tiles,
                             alpha=alpha)
    col = lambda cb: pl.BlockSpec((TM, d), lambda i: (i, cb))
    rpt = TM // hr
    n_hblk = proj.shape[0] // hr
    prev = lambda cb: pl.BlockSpec((hr, d), lambda i: (jnp.maximum(i * rpt - 1, 0), cb))
    nxt = lambda cb: pl.BlockSpec((hr, d), lambda i: (jnp.minimum((i + 1) * rpt, n_hblk - 1), cb))
    const = lambda shape: pl.BlockSpec(shape, lambda i: (0,) * len(shape), pipeline_mode=pl.Buffered(1))
    if two_src:
        x_specs = [pl.BlockSpec((TM, d), lambda i: (jnp.minimum(i, n_lat_tiles - 1), 0)),
                   pl.BlockSpec((TM, d), lambda i: (jnp.maximum(i - n_lat_tiles, 0), 0))]
        x_args = (x_lat, x_ctx)
    else:
        x_specs = [pl.BlockSpec((TM, d), lambda i: (i, 0))]
        x_args = (x_lat,)
    return pl.pallas_call(
        kern,
        grid=(n_rows // TM,),
        in_specs=x_specs + [pl.BlockSpec((TM, attn.shape[1]), lambda i: (i, 0)),
                            col(0), col(1), col(2), col(3), col(4),
                            prev(1), prev(2), nxt(1), nxt(2),
                            pl.BlockSpec((1, 6, d), lambda i: (seg_of_row_tile(i, TM), 0, 0)),
                            const((3, d)), const(wa.shape), const(wc.shape), const(wo.shape),
                            const((1, d)), const((1, d))],
        out_specs=pl.BlockSpec((TM, d), lambda i: (i, 0)),
        out_shape=jax.ShapeDtypeStruct((n_rows, d), F32),
        compiler_params=_cparams("parallel"),
        name="merge_ln1",
    )(*x_args, attn, proj, proj, proj, proj, proj, proj, proj, proj, proj, mod, conv_w, wa, wc, wo,
      g.reshape(1, d), b.reshape(1, d))


def _router_kernel(x_ref, mod_ref, rwt_ref, rb_ref, e_ref, w_ref, rank_ref, cnt_ref, cnt_scr, *, n_groups):
    i = pl.program_id(0)
    tm = x_ref.shape[0]
    n_exp = rwt_ref.shape[0]
    per = n_exp // n_groups

    @pl.when(i == 0)
    def _():
        cnt_scr[...] = jnp.zeros_like(cnt_scr)

    h2 = x_ref[...] * (1.0 + mod_ref[0, 4:5, :]) + mod_ref[0, 3:4, :]
    logits = lax.dot_general(rwt_ref[...].astype(BF16), h2.astype(BF16), NT_DIMS,
                             preferred_element_type=F32)
    ex = jnp.exp(logits - jnp.max(logits, axis=0, keepdims=True))
    probs = ex / jnp.sum(ex, axis=0, keepdims=True)
    sel3 = (probs + rb_ref[...]).reshape(n_groups, per, tm)

    li = lax.broadcasted_iota(jnp.int32, (n_groups, per, tm), 1).astype(F32)
    m1 = jnp.max(sel3, axis=1, keepdims=True)
    i1 = jnp.min(jnp.where(sel3 == m1, li, float(per)), axis=1, keepdims=True)
    sel3b = jnp.where(li == i1, -jnp.inf, sel3)
    m2 = jnp.max(sel3b, axis=1, keepdims=True)
    i2 = jnp.min(jnp.where(sel3b == m2, li, float(per)), axis=1, keepdims=True)
    score = m1 + m2
    gi = lax.broadcasted_iota(jnp.int32, (n_groups, 1, tm), 0).astype(F32)
    gbest = jnp.min(jnp.where(score == jnp.max(score, axis=0, keepdims=True), gi, float(n_groups)),
                    axis=0, keepdims=True)
    pick = gi == gbest
    l1 = jnp.sum(jnp.where(pick, i1, 0.0), axis=0)
    l2 = jnp.sum(jnp.where(pick, i2, 0.0), axis=0)
    e1 = gbest[0] * per + l1
    e2 = gbest[0] * per + l2

    ei = lax.broadcasted_iota(jnp.int32, (n_exp, tm), 0).astype(F32)
    is1 = ei == e1
    is2 = ei == e2
    p1 = jnp.sum(jnp.where(is1, probs, 0.0), axis=0, keepdims=True)
    p2 = jnp.sum(jnp.where(is2, probs, 0.0), axis=0, keepdims=True)
    psum = p1 + p2

    onehot = jnp.where(jnp.logical_or(is1, is2), 1.0, 0.0)
    srow = lax.broadcasted_iota(jnp.int32, (tm, tm), 0)
    scol = lax.broadcasted_iota(jnp.int32, (tm, tm), 1)
    before = jnp.where(srow < scol, 1.0, 0.0).astype(BF16)
    excl = jnp.dot(onehot.astype(BF16), before, preferred_element_type=F32) + cnt_scr[:, 0:1]
    r1 = jnp.sum(jnp.where(is1, excl, 0.0), axis=0, keepdims=True)
    r2 = jnp.sum(jnp.where(is2, excl, 0.0), axis=0, keepdims=True)
    cnt_scr[...] = cnt_scr[...] + jnp.sum(onehot, axis=1, keepdims=True)

    e_ref[0:1, :] = e1.astype(jnp.int32)
    e_ref[1:2, :] = e2.astype(jnp.int32)
    w_ref[0:1, :] = p1 / psum
    w_ref[1:2, :] = p2 / psum
    rank_ref[0:1, :] = r1.astype(jnp.int32)
    rank_ref[1:2, :] = r2.astype(jnp.int32)
    cnt_ref[...] = cnt_scr[...]


def _router(x1, mod, rwt, rb, *, seg_of_row_tile, d):
    r = x1.shape[0]
    n_exp = rwt.shape[0]
    row2 = lambda dt: jax.ShapeDtypeStruct((TOP_K, r), dt)
    spec2 = pl.BlockSpec((TOP_K, TM_ROUTER), lambda i: (0, i))
    assert r % TM_ROUTER == 0
    return pl.pallas_call(
        functools.partial(_router_kernel, n_groups=N_EXPERT_GROUPS),
        grid=(r // TM_ROUTER,),
        in_specs=[pl.BlockSpec((TM_ROUTER, d), lambda i: (i, 0)),
                  pl.BlockSpec((1, 6, d), lambda i: (seg_of_row_tile(i, TM_ROUTER), 0, 0)),
                  pl.BlockSpec((n_exp, d), lambda i: (0, 0)),
                  pl.BlockSpec((n_exp, 1), lambda i: (0, 0))],
        out_specs=[spec2, spec2, spec2, pl.BlockSpec((n_exp, LANE), lambda i: (0, 0))],
        out_shape=[row2(jnp.int32), row2(F32), row2(jnp.int32), jax.ShapeDtypeStruct((n_exp, LANE), F32)],
        scratch_shapes=[pltpu.VMEM((n_exp, LANE), F32)],
        compiler_params=_cparams("arbitrary"),
        name="router",
    )(x1, mod, rwt, rb.reshape(n_exp, 1))


def _dispatch_kernel(dest_ref, fill_lo_ref, fill_hi_ref, x_ref, mod_ref, xs_ref, buf, zslab, sem, zsem, *, n_rows):
    i = pl.program_id(0)
    n_steps = pl.num_programs(0)
    tm = x_ref.shape[0]
    words = _pack_rows(x_ref[...] * (1.0 + mod_ref[0, 4:5, :]) + mod_ref[0, 3:4, :])

    def drain():
        for _ in range(TOP_K):
            pltpu.make_async_copy(buf, xs_ref.at[pl.ds(0, tm * SUBLANE)], sem).wait()

    @pl.when(i > 0)
    def _():
        drain()

    for s, w in enumerate(words):
        buf[pl.ds(s, tm, stride=SUBLANE), :] = w

    def issue(t, carry):
        src = buf.at[pl.ds(pl.multiple_of(t * SUBLANE, SUBLANE), SUBLANE)]
        for k in range(TOP_K):
            dst = dest_ref[k * n_rows + i * tm + t]
            pltpu.make_async_copy(src, xs_ref.at[pl.ds(pl.multiple_of(dst * SUBLANE, SUBLANE), SUBLANE)],
                                  sem).start(priority=k % 2)
        return carry

    lax.fori_loop(0, tm, issue, 0, unroll=ISSUE_UNROLL)

    @pl.when(i == n_steps - 1)
    def _():
        drain()
        zslab[...] = jnp.zeros_like(zslab)
        n_ranges = fill_lo_ref.shape[0]

        def zero_copy(rr):
            return pltpu.make_async_copy(zslab, xs_ref.at[pl.ds(pl.multiple_of(rr * SUBLANE, SUBLANE), SUBLANE)], zsem)

        def per_range(fn):
            def body(e, carry):
                lax.fori_loop(fill_lo_ref[e], fill_hi_ref[e], lambda rr, c: (fn(rr), c)[1], 0)
                return carry
            lax.fori_loop(0, n_ranges, body, 0)

        per_range(lambda rr: zero_copy(rr).start())
        per_range(lambda rr: zero_copy(rr).wait())


def _dispatch(x1, mod, dest_flat, fill_lo, fill_hi, cap, *, seg_of_row_tile, d):
    r = x1.shape[0]
    slab = _slab_rows(d)
    grid_spec = pltpu.PrefetchScalarGridSpec(
        num_scalar_prefetch=3,
        grid=(r // TM,),
        in_specs=[pl.BlockSpec((TM, d), lambda i, *_: (i, 0)),
                  pl.BlockSpec((1, 6, d), lambda i, *_: (seg_of_row_tile(i, TM), 0, 0))],
        out_specs=pl.BlockSpec(memory_space=pl.ANY),
        scratch_shapes=[pltpu.VMEM((TM * slab, LANE), U32), pltpu.VMEM((slab, LANE), U32),
                        pltpu.SemaphoreType.DMA(()), pltpu.SemaphoreType.DMA(())],
    )
    return pl.pallas_call(
        functools.partial(_dispatch_kernel, n_rows=r),
        grid_spec=grid_spec,
        out_shape=jax.ShapeDtypeStruct((cap * slab, LANE), U32),
        compiler_params=_cparams("arbitrary"),
        name="moe_dispatch",
    )(dest_flat, fill_lo, fill_hi, x1, mod)


def _expert_kernel(blk_e_ref, n_used_ref, xs_ref, wg_ref, wu_ref, wd_ref, o_ref):
    b = pl.program_id(0)
    mb = xs_ref.shape[0] // SUBLANE

    @pl.when(b < n_used_ref[0])
    def _():
        xb = _unpack_rows([xs_ref[pl.ds(s, mb, stride=SUBLANE), :] for s in range(SUBLANE)]).astype(BF16)
        g = jnp.dot(xb, wg_ref[...], preferred_element_type=F32)
        u = jnp.dot(xb, wu_ref[...], preferred_element_type=F32)
        hmid = (g * jax.nn.sigmoid(g) * u).astype(BF16)
        y = jnp.dot(hmid, wd_ref[...], preferred_element_type=F32)
        for s, w in enumerate(_pack_rows(y)):
            o_ref[pl.ds(s, mb, stride=SUBLANE), :] = w

    @pl.when(b >= n_used_ref[0])
    def _():
        o_ref[...] = jnp.zeros_like(o_ref)


def _experts(xs, blk_expert, n_used, wg, wu, wd):
    d, ff = wg.shape[1], wg.shape[2]
    slab = _slab_rows(d)
    n_blk = xs.shape[0] // (MOE_BLOCK * slab)
    xrow = lambda b, be, nu: (jnp.minimum(b, nu[0] - 1), 0)
    grid_spec = pltpu.PrefetchScalarGridSpec(
        num_scalar_prefetch=2,
        grid=(n_blk,),
        in_specs=[pl.BlockSpec((MOE_BLOCK * slab, LANE), xrow),
                  pl.BlockSpec((None, d, ff), lambda b, be, nu: (be[b], 0, 0)),
                  pl.BlockSpec((None, d, ff), lambda b, be, nu: (be[b], 0, 0)),
                  pl.BlockSpec((None, ff, d), lambda b, be, nu: (be[b], 0, 0))],
        out_specs=pl.BlockSpec((MOE_BLOCK * slab, LANE), lambda b, be, nu: (b, 0)),
    )
    return pl.pallas_call(
        _expert_kernel,
        grid_spec=grid_spec,
        out_shape=jax.ShapeDtypeStruct(xs.shape, U32),
        compiler_params=_cparams("arbitrary"),
        name="moe_experts",
    )(blk_expert, n_used, xs, wg, wu, wd)


def _combine_kernel(*refs, n_rows, alpha, emit_h):
    bufs, sems = refs[-COMBINE_SLOTS - 1:-1], refs[-1]
    refs = refs[:-COMBINE_SLOTS - 1]
    if emit_h:
        dest_ref, x_ref, wt_ref, mod_ref, g_ref, b_ref, modn_ref, ys_ref, o_ref, h_ref = refs
    else:
        dest_ref, x_ref, wt_ref, mod_ref, g_ref, b_ref, ys_ref, o_ref = refs
    i = pl.program_id(0)
    n_steps = pl.num_programs(0)
    tm = x_ref.shape[0]
    n_slots = len(bufs)
    ahead = n_slots - 1
    slot = i % n_slots

    def gather(tile, sl, t, k):
        src = dest_ref[k * n_rows + tile * tm + t]
        row0 = t * SUBLANE if isinstance(t, int) else pl.multiple_of(t * SUBLANE, SUBLANE)
        return pltpu.make_async_copy(ys_ref.at[pl.ds(pl.multiple_of(src * SUBLANE, SUBLANE), SUBLANE)],
                                     bufs[sl].at[k, pl.ds(row0, SUBLANE)], sems.at[sl])

    def wait_tile(sl):
        for k in range(TOP_K):
            pltpu.make_async_copy(ys_ref.at[pl.ds(0, tm * SUBLANE)], bufs[sl].at[k], sems.at[sl]).wait()

    @pl.when(i == 0)
    def _():
        for j in range(ahead):
            def body(t, carry, j=j):
                for k in range(TOP_K):
                    gather(jnp.minimum(j, n_steps - 1), j, t, k).start(priority=k % 2)
                return carry
            lax.fori_loop(0, tm, body, 0, unroll=ISSUE_UNROLL)

    def finish(sl):
        wait_tile(sl)
        nxt = jnp.minimum(i + ahead, n_steps - 1)
        for t in range(tm):
            for k in range(TOP_K):
                gather(nxt, (sl + ahead) % n_slots, t, k).start(priority=k % 2)
        y = [_unpack_rows([bufs[sl][k, pl.ds(s, tm, stride=SUBLANE), :] for s in range(SUBLANE)])
             for k in range(TOP_K)]
        f = wt_ref[:, 0:1] * y[0] + wt_ref[:, 1:2] * y[1]
        z = alpha * x_ref[...] + mod_ref[0, 5:6, :] * f
        x2 = _layer_norm(z, g_ref[...], b_ref[...])
        o_ref[...] = x2
        if emit_h:
            h_ref[...] = (x2 * (1.0 + modn_ref[0, 1:2, :]) + modn_ref[0, 0:1, :]).astype(BF16)

    for sl in range(n_slots):
        pl.when(slot == sl)(functools.partial(finish, sl))

    for sl in range(n_slots):
        pl.when(jnp.logical_and(i == n_steps - 1, slot != sl))(functools.partial(wait_tile, sl))


def _combine(x1, wt_rows, mod, g, b, dest_flat, ys, mod_next, *, seg_of_row_tile, d, alpha):
    r = x1.shape[0]
    slab = _slab_rows(d)
    emit_h = mod_next is not None
    mod_spec = pl.BlockSpec((1, 6, d), lambda i, *_: (seg_of_row_tile(i, TM), 0, 0))
    row_spec = pl.BlockSpec((TM, d), lambda i, *_: (i, 0))
    vec_spec = pl.BlockSpec((1, d), lambda i, *_: (0, 0))
    in_specs = [row_spec, pl.BlockSpec((TM, TOP_K), lambda i, *_: (i, 0)), mod_spec, vec_spec, vec_spec]
    args = [x1, wt_rows, mod, g.reshape(1, d), b.reshape(1, d)]
    out_specs, out_shape = [row_spec], [jax.ShapeDtypeStruct((r, d), F32)]
    if emit_h:
        in_specs.append(mod_spec)
        args.append(mod_next)
        out_specs.append(row_spec)
        out_shape.append(jax.ShapeDtypeStruct((r, d), BF16))
    in_specs.append(pl.BlockSpec(memory_space=pl.ANY))
    args.append(ys)
    grid_spec = pltpu.PrefetchScalarGridSpec(
        num_scalar_prefetch=1,
        grid=(r // TM,),
        in_specs=in_specs,
        out_specs=out_specs,
        scratch_shapes=[pltpu.VMEM((TOP_K, TM * slab, LANE), U32) for _ in range(COMBINE_SLOTS)]
        + [pltpu.SemaphoreType.DMA((COMBINE_SLOTS,))],
    )
    outs = pl.pallas_call(
        functools.partial(_combine_kernel, n_rows=r, alpha=alpha, emit_h=emit_h),
        grid_spec=grid_spec,
        out_shape=out_shape,
        compiler_params=_cparams("arbitrary"),
        name="moe_combine_ln2",
    )(dest_flat, *args)
    return (outs[0], outs[1]) if emit_h else (outs[0], None)


def _rope_tables(n_batch, n_seq, ctx_rows):
    nf = HEAD_DIM // 4
    inv_freq = np.power(np.float32(ROPE_BASE), -np.arange(nf, dtype=np.float32) / np.float32(nf)).astype(np.float32)
    n_grid_rows = n_seq // GRID_W
    ar = (np.arange(n_grid_rows, dtype=np.float32)[:, None] * inv_freq).astype(np.float32)
    ac = (np.arange(GRID_W, dtype=np.float32)[:, None] * inv_freq).astype(np.float32)
    shape = (n_grid_rows, GRID_W, nf)
    by_row = lambda a: jnp.broadcast_to(jnp.asarray(a, F32)[:, None, :], shape)
    by_col = lambda a: jnp.broadcast_to(jnp.asarray(a, F32)[None, :, :], shape)
    cr, sr, cc, sc = np.cos(ar), np.sin(ar), np.cos(ac), np.sin(ac)
    cos_t = jnp.concatenate([by_row(cr), by_row(cr), by_col(cc), by_col(cc)], axis=-1).reshape(n_seq, HEAD_DIM)
    sin_t = jnp.concatenate([by_row(-sr), by_row(sr), by_col(-sc), by_col(sc)], axis=-1).reshape(n_seq, HEAD_DIM)
    cos_t = jnp.concatenate([cos_t] * n_batch + [jnp.ones((ctx_rows, HEAD_DIM), F32)], axis=0)
    sin_t = jnp.concatenate([sin_t] * n_batch + [jnp.zeros((ctx_rows, HEAD_DIM), F32)], axis=0)
    return cos_t, sin_t


def kernel(x, c, ctx, c_ctx, w_ada, b_ada, w_in, attn_sink, conv_w, w_attn_proj, w_conv_proj, w_out,
           ln1_g, ln1_b, ln2_g, ln2_b, router_w, router_b, w_gate, w_up, w_down):
    n_batch, n_seq, d = x.shape
    n_ctx = ctx.shape[1]
    depth = w_ada.shape[0]
    n_exp = router_w.shape[1]
    q_width = attn_sink.shape[1] * HEAD_DIM
    in_width = w_in.shape[2]
    kv_width = (in_width - q_width - 5 * d) // 2
    t = n_batch * n_seq
    r = t + n_batch * n_ctx
    alpha = float((2 * depth) ** 0.25)
    assert w_conv_proj.shape[1] == d and n_seq % GRID_W == 0
    assert r % TM_IN == 0 and r % TM == 0 and r % TQ == 0

    def seg_of_row_tile(i, tile):
        return jnp.minimum((i * tile) // n_seq, n_batch)

    x_lat, x_ctx = x.reshape(t, d), ctx.reshape(n_batch * n_ctx, d)
    rope_c, rope_s = _rope_tables(n_batch, n_seq, n_batch * n_ctx)
    c_rows = jnp.concatenate([c, c_ctx[None, :], jnp.zeros((8 - n_batch - 1, d), F32)], axis=0)
    mods = [_mod_table(c_rows, w_ada, b_ada[l], l).reshape(8, 6, d) for l in range(depth)]
    rwt = router_w.T
    q_col, k_col, v_col = 5 * d, 5 * d + q_width, 5 * d + q_width + kv_width
    expert_ids = jnp.arange(n_exp, dtype=jnp.int32)

    h = _modulate(x_lat, x_ctx, mods[0], seg_of_row_tile=seg_of_row_tile)
    x_cur = None
    for l in range(depth):
        last = l == depth - 1
        mod = mods[l]
        n_rows = t if last else r
        proj, wg_b, wu_b, wd_b = _in_proj(h, w_in, l, rope_c, rope_s, w_gate, w_up, w_down, d=d, q_width=q_width,
                                          kv_width=kv_width)
        attn = _attention(proj, attn_sink[l], n_rows=n_rows, n_batch=n_batch, n_seq=n_seq, n_ctx=n_ctx, q_col=q_col,
                          k_col=k_col, v_col=v_col, q_width=q_width, kv_width=kv_width)
        src = (x_lat, x_ctx) if l == 0 else (x_cur, None)
        x1 = _merge(src[0], src[1], attn, proj, mod, conv_w[l], w_attn_proj[l].astype(BF16),
                    w_conv_proj[l].astype(BF16), w_out[l].astype(BF16), ln1_g[l], ln1_b[l], n_rows=n_rows,
                    seg_of_row_tile=seg_of_row_tile, d=d, n_seq=n_seq, n_ctx=n_ctx, n_batch=n_batch, alpha=alpha)

        e_idx, wts, rank, cnt = _router(x1, mod, rwt, router_b, seg_of_row_tile=seg_of_row_tile, d=d)
        cap = (n_rows * TOP_K + n_exp * (MOE_BLOCK - 1) + MOE_BLOCK - 1) // MOE_BLOCK * MOE_BLOCK
        n_blk = cap // MOE_BLOCK
        counts = cnt[:, 0].astype(jnp.int32)
        padded = (counts + MOE_BLOCK - 1) // MOE_BLOCK * MOE_BLOCK
        pend = jnp.cumsum(padded)
        pstart = pend - padded
        slot0 = jnp.sum(jnp.where(e_idx[:, :, None] == expert_ids, pstart, 0), axis=-1)
        dest_flat = (slot0 + rank).reshape(-1)
        n_used = (pend[-1] // MOE_BLOCK).astype(jnp.int32)
        blk_start = jnp.minimum(jnp.arange(n_blk, dtype=jnp.int32), n_used - 1) * MOE_BLOCK
        blk_expert = jnp.sum(blk_start[:, None] >= pend[None, :], axis=-1).astype(jnp.int32)
        fill_lo = jnp.concatenate([pstart + counts, pend[-1:]]).astype(jnp.int32)
        fill_hi = jnp.concatenate([pend, jnp.full((1,), cap, pend.dtype)]).astype(jnp.int32)
        xs = _dispatch(x1, mod, dest_flat, fill_lo, fill_hi, cap, seg_of_row_tile=seg_of_row_tile, d=d)
        ys = _experts(xs, blk_expert, n_used.reshape(1), wg_b, wu_b, wd_b)
        x_cur, h = _combine(x1, wts.T, mod, ln2_g[l], ln2_b[l], dest_flat, ys, None if last else mods[l + 1],
                            seg_of_row_tile=seg_of_row_tile, d=d, alpha=alpha)

    return x_cur.reshape(n_batch, n_seq, d)
```

```python
import functools

import numpy as np
import jax
import jax.numpy as jnp
from jax import lax
from jax.experimental import pallas as pl
from jax.experimental.pallas import tpu as pltpu

F32 = jnp.float32
BF16 = jnp.bfloat16
U32 = jnp.uint32
HIGHEST = lax.Precision.HIGHEST

GRID_W = 64
HEAD_DIM = 128
WINDOW = 128
ROPE_BASE = 10000.0
N_EXPERT_GROUPS = 4
TOP_K = 2
LN_EPS = 1e-5
NEG_INF = -1e30

LANE = 128
SUBLANE = 8
TQ = WINDOW
ATTN_TILES = 2
TM = 256
TM_IN = 1280
TN_IN = 1024
TN_CHUNK = 256
EXPERT_CAST_BLOCKS = 256
TM_ROUTER = 512
TN_MOD = 1024
MOE_BLOCK = 256
ISSUE_UNROLL = 8
COMBINE_SLOTS = 3
VMEM_LIMIT = 56 * 1024 * 1024

NT_DIMS = (((1,), (1,)), ((), ()))
TN_DIMS = (((0,), (0,)), ((), ()))
LOG2E = 1.4426950408889634


def _cparams(*sem):
    return pltpu.CompilerParams(dimension_semantics=sem, vmem_limit_bytes=VMEM_LIMIT)


def _layer_norm(z, g, b):
    mu = jnp.mean(z, axis=-1, keepdims=True)
    zc = z - mu
    var = jnp.mean(zc * zc, axis=-1, keepdims=True)
    return zc * lax.rsqrt(var + LN_EPS) * g + b


def _pack_rows(v):
    half = v.shape[1] // 2
    bits = lax.bitcast_convert_type(v.astype(BF16).astype(F32), U32)
    words = []
    for s in range(half // LANE):
        lo = bits[:, s * LANE:(s + 1) * LANE] >> 16
        hi = bits[:, half + s * LANE:half + (s + 1) * LANE] & np.uint32(0xFFFF0000)
        words.append(lo | hi)
    return words


def _unpack_rows(words):
    lo = [lax.bitcast_convert_type(w << 16, F32) for w in words]
    hi = [lax.bitcast_convert_type(w & np.uint32(0xFFFF0000), F32) for w in words]
    return jnp.concatenate(lo + hi, axis=1)


def _slab_rows(d):
    assert d % (2 * LANE) == 0 and d // (2 * LANE) == SUBLANE, "one row must pack into one (8, 128) uint32 tile"
    return d // (2 * LANE)


def _mod_kernel(c_ref, w_ref, b_ref, o_ref):
    cc = c_ref[...]
    s = cc * jax.nn.sigmoid(cc)
    o_ref[...] = jnp.dot(s, w_ref[...], preferred_element_type=F32, precision=HIGHEST) + b_ref[...]


def _mod_table(c_rows, w_ada, b_ada_l, layer):
    rows, d = c_rows.shape
    n_out = w_ada.shape[2]
    return pl.pallas_call(
        _mod_kernel,
        grid=(n_out // TN_MOD,),
        in_specs=[pl.BlockSpec((rows, d), lambda j: (0, 0)),
                  pl.BlockSpec((None, d, TN_MOD), lambda j: (layer, 0, j)),
                  pl.BlockSpec((1, TN_MOD), lambda j: (0, j))],
        out_specs=pl.BlockSpec((rows, TN_MOD), lambda j: (0, j)),
        out_shape=jax.ShapeDtypeStruct((rows, n_out), F32),
        compiler_params=_cparams("parallel"),
        name="mod_table",
    )(c_rows, w_ada, b_ada_l.reshape(1, n_out))


def _modulate_kernel(xa_ref, xb_ref, mod_ref, o_ref, *, n_lat_tiles):
    x = jnp.where(pl.program_id(0) < n_lat_tiles, xa_ref[...], xb_ref[...])
    o_ref[...] = (x * (1.0 + mod_ref[0, 1:2, :]) + mod_ref[0, 0:1, :]).astype(BF16)


def _modulate(x_lat, x_ctx, mod, *, seg_of_row_tile):
    t, d = x_lat.shape
    r = t + x_ctx.shape[0]
    n_lat = t // TM
    return pl.pallas_call(
        functools.partial(_modulate_kernel, n_lat_tiles=n_lat),
        grid=(r // TM,),
        in_specs=[pl.BlockSpec((TM, d), lambda i: (jnp.minimum(i, n_lat - 1), 0)),
                  pl.BlockSpec((TM, d), lambda i: (jnp.maximum(i - n_lat, 0), 0)),
                  pl.BlockSpec((1, 6, d), lambda i: (seg_of_row_tile(i, TM), 0, 0))],
        out_specs=pl.BlockSpec((TM, d), lambda i: (i, 0)),
        out_shape=jax.ShapeDtypeStruct((r, d), BF16),
        compiler_params=_cparams("parallel"),
        name="modulate_in",
    )(x_lat, x_ctx, mod)


def _inproj_kernel(h_ref, w_ref, rc_ref, rs_ref, eg_ref, eu_ref, ed_ref, o_ref, ego_ref, euo_ref, edo_ref, wb_scr, *,
                   n_plain, n_gate, n_q, k_cols, scale):
    j = pl.program_id(0)
    tn = o_ref.shape[1]

    @pl.when(pl.program_id(1) == 0)
    def _():
        wb_scr[...] = w_ref[...].astype(BF16)

    def rope(xh):
        lane = lax.broadcasted_iota(jnp.int32, xh.shape, 1)
        sw = jnp.where((lane & 32) == 0, pltpu.roll(xh, LANE - 32, 1), pltpu.roll(xh, 32, 1))
        return xh * rc_ref[...] + sw * rs_ref[...]

    def column_chunks(epilogue):
        ego_ref[...] = eg_ref[...].astype(BF16)
        euo_ref[...] = eu_ref[...].astype(BF16)
        edo_ref[...] = ed_ref[...].astype(BF16)
        for c0 in range(0, tn, TN_CHUNK):
            acc = jnp.dot(h_ref[...], wb_scr[:, c0:c0 + TN_CHUNK], preferred_element_type=F32)
            for c in range(c0, c0 + TN_CHUNK, LANE):
                o_ref[:, c:c + LANE] = epilogue(acc[:, c - c0:c - c0 + LANE], c).astype(BF16)

    @pl.when(j < n_plain)
    def _():
        column_chunks(lambda a, c: a)

    @pl.when((j >= n_plain) & (j < n_plain + n_gate))
    def _():
        column_chunks(lambda a, c: jax.nn.sigmoid(a))

    @pl.when((j >= n_plain + n_gate) & (j < n_plain + n_gate + n_q))
    def _():
        column_chunks(lambda a, c: rope(a) * scale)

    @pl.when(j == n_plain + n_gate + n_q)
    def _():
        column_chunks(lambda a, c: rope(a) if c < k_cols else a)


def _in_proj(h, w_in, layer, rope_c, rope_s, w_gate, w_up, w_down, *, d, q_width, kv_width):
    r = h.shape[0]
    in_width = w_in.shape[2]
    base = q_width + 2 * kv_width
    assert (3 * d) % TN_IN == 0 and (2 * d) % TN_IN == 0 and q_width % TN_IN == 0 and 2 * kv_width == TN_IN
    n_plain, n_gate, n_q = 3 * d // TN_IN, 2 * d // TN_IN, q_width // TN_IN
    n_j = in_width // TN_IN
    assert n_plain + n_gate + n_q + 1 == n_j and base % TN_IN == 0 and r % TM_IN == 0
    shift = base // TN_IN

    kern = functools.partial(_inproj_kernel, n_plain=n_plain, n_gate=n_gate, n_q=n_q, k_cols=kv_width,
                             scale=HEAD_DIM ** -0.5 * LOG2E)
    n_i = r // TM_IN
    depth, n_exp, _, ff = w_gate.shape
    assert n_j * n_i >= EXPERT_CAST_BLOCKS
    rows_gu, rows_d = n_exp * d // EXPERT_CAST_BLOCKS, n_exp * ff // EXPERT_CAST_BLOCKS
    assert rows_gu * EXPERT_CAST_BLOCKS == n_exp * d and rows_d * EXPERT_CAST_BLOCKS == n_exp * ff
    assert rows_gu % 16 == 0 and rows_d % 16 == 0

    def cast_blk(j, i):
        return jnp.minimum(j * n_i + i, EXPERT_CAST_BLOCKS - 1)

    cast_in = lambda rows, cols: pl.BlockSpec((None, rows, cols), lambda j, i: (layer, cast_blk(j, i), 0))
    cast_out = lambda rows, cols: pl.BlockSpec((rows, cols), lambda j, i: (cast_blk(j, i), 0))
    proj, wg_b, wu_b, wd_b = pl.pallas_call(
        kern,
        grid=(n_j, n_i),
        in_specs=[pl.BlockSpec((TM_IN, d), lambda j, i: (i, 0)),
                  pl.BlockSpec((None, d, TN_IN), lambda j, i: (layer, 0, (j + shift) % n_j)),
                  pl.BlockSpec((TM_IN, LANE), lambda j, i: (i, 0)),
                  pl.BlockSpec((TM_IN, LANE), lambda j, i: (i, 0)),
                  cast_in(rows_gu, ff), cast_in(rows_gu, ff), cast_in(rows_d, d)],
        out_specs=[pl.BlockSpec((TM_IN, TN_IN), lambda j, i: (i, j)),
                   cast_out(rows_gu, ff), cast_out(rows_gu, ff), cast_out(rows_d, d)],
        out_shape=[jax.ShapeDtypeStruct((r, in_width), BF16),
                   jax.ShapeDtypeStruct((n_exp * d, ff), BF16), jax.ShapeDtypeStruct((n_exp * d, ff), BF16),
                   jax.ShapeDtypeStruct((n_exp * ff, d), BF16)],
        scratch_shapes=[pltpu.VMEM((d, TN_IN), BF16)],
        compiler_params=_cparams("arbitrary", "arbitrary"),
        name="in_proj",
    )(h, w_in, rope_c, rope_s, w_gate.reshape(depth, n_exp * d, ff), w_up.reshape(depth, n_exp * d, ff),
      w_down.reshape(depth, n_exp * ff, d))
    return proj, wg_b.reshape(n_exp, d, ff), wu_b.reshape(n_exp, d, ff), wd_b.reshape(n_exp, ff, d)


def _attn_kernel(sink_ref, q_ref, kp_ref, kc_ref, kn_ref, vp_ref, vc_ref, vn_ref, kx_ref, vx_ref, o_ref, *,
                 nb_seq, n_lat_tiles, n_kv, group):
    i = pl.program_id(0)
    tq = kp_ref.shape[0]
    is_lat = i < n_lat_tiles
    p = i % nb_seq
    n_ctx = kx_ref.shape[0]
    key = lax.broadcasted_iota(jnp.int32, (tq, tq), 0)
    qry = lax.broadcasted_iota(jnp.int32, (tq, tq), 1)
    qrow = lax.broadcasted_iota(jnp.int32, (group * tq, tq), 0)
    qcol = lax.broadcasted_iota(jnp.int32, (group * tq, tq), 1)
    row_onehot = jnp.where(qrow % tq == qcol, 1.0, 0.0).astype(BF16)
    n_sub = kc_ref.shape[0] // tq
    for sub in range(n_sub):
        _attn_tile(sink_ref, q_ref, o_ref, kx_ref, vx_ref, sub * tq, key, qry, row_onehot, is_lat,
                   has_prev=jnp.logical_and(is_lat, p > 0) if sub == 0 else is_lat,
                   has_next=jnp.logical_and(is_lat, p < nb_seq - 1) if sub == n_sub - 1 else is_lat,
                   k_prev=kp_ref if sub == 0 else kc_ref.at[(sub - 1) * tq:sub * tq],
                   k_cur=kc_ref.at[sub * tq:(sub + 1) * tq],
                   k_next=kn_ref if sub == n_sub - 1 else kc_ref.at[(sub + 1) * tq:(sub + 2) * tq],
                   v_prev=vp_ref if sub == 0 else vc_ref.at[(sub - 1) * tq:sub * tq],
                   v_cur=vc_ref.at[sub * tq:(sub + 1) * tq],
                   v_next=vn_ref if sub == n_sub - 1 else vc_ref.at[(sub + 1) * tq:(sub + 2) * tq],
                   n_kv=n_kv, group=group, n_ctx=n_ctx)


def _attn_tile(sink_ref, q_ref, o_ref, kx_ref, vx_ref, r0, key, qry, row_onehot, is_lat, *, has_prev, has_next,
               k_prev, k_cur, k_next, v_prev, v_cur, v_next, n_kv, group, n_ctx):
    tq = key.shape[0]
    b_prev = jnp.where(jnp.logical_and(key >= qry, has_prev), 0.0, NEG_INF)
    b_cur = jnp.where(jnp.logical_and(key >= 0, is_lat), 0.0, NEG_INF)
    b_next = jnp.where(jnp.logical_and(key <= qry, has_next), 0.0, NEG_INF)
    bias_t = jnp.concatenate([b_prev, b_cur, b_next, jnp.zeros((n_ctx, tq), F32)], axis=0).astype(BF16)
    rows = slice(r0, r0 + tq)

    for h in range(n_kv):
        hs = slice(h * HEAD_DIM, (h + 1) * HEAD_DIM)
        q4 = jnp.concatenate([q_ref[rows, (h * group + g) * HEAD_DIM:(h * group + g + 1) * HEAD_DIM]
                              for g in range(group)], axis=0)
        k_all = jnp.concatenate([k_prev[:, hs], k_cur[:, hs], k_next[:, hs], kx_ref[:, hs]], axis=0)
        v_all = jnp.concatenate([v_prev[:, hs], v_cur[:, hs], v_next[:, hs], vx_ref[:, hs]], axis=0)
        s_t = lax.dot_general(jnp.concatenate([k_all, bias_t], axis=1), jnp.concatenate([q4, row_onehot], axis=1),
                              NT_DIMS, preferred_element_type=F32)
        p_cols, inv_cols = [], []
        for g in range(group):
            sg = s_t[:, g * tq:(g + 1) * tq]
            sink2 = sink_ref[h * group + g] * LOG2E
            m = jnp.maximum(jnp.max(sg, axis=0, keepdims=True), sink2)
            p = jnp.exp2(sg - m)
            denom = jnp.sum(p, axis=0, keepdims=True) + jnp.exp2(sink2 - m)
            p_cols.append(p.astype(BF16))
            inv_cols.append(1.0 / denom)
        o_t = lax.dot_general(v_all, jnp.concatenate(p_cols, axis=1), TN_DIMS, preferred_element_type=F32)
        o_t = o_t * jnp.concatenate(inv_cols, axis=1)
        for g in range(group):
            hq = h * group + g
            o_ref[rows, hq * HEAD_DIM:(hq + 1) * HEAD_DIM] = o_t[:, g * tq:(g + 1) * tq].T.astype(BF16)


def _attention(proj, sink, *, n_rows, n_batch, n_seq, n_ctx, q_col, k_col, v_col, q_width, kv_width):
    t = n_batch * n_seq
    ts = ATTN_TILES * TQ
    nb_seq = n_seq // ts
    n_lat_tiles = t // ts
    n_kv = kv_width // HEAD_DIM
    group = q_width // kv_width
    assert q_col % q_width == 0 and k_col % kv_width == 0 and v_col % kv_width == 0
    assert n_seq % ts == 0 and n_ctx % ts == 0 and n_rows % ts == 0 and t % n_ctx == 0
    ctx_steps = n_ctx // ts
    kcb, vcb, qcb = k_col // kv_width, v_col // kv_width, q_col // q_width

    def lat(i):
        return i < n_lat_tiles

    def prev_idx(i):
        return jnp.where(jnp.logical_and(lat(i), i % nb_seq > 0), i * ATTN_TILES - 1, i * ATTN_TILES)

    def next_idx(i):
        return jnp.where(jnp.logical_and(lat(i), i % nb_seq < nb_seq - 1), (i + 1) * ATTN_TILES, i * ATTN_TILES)

    def ctx_idx(i):
        b = jnp.where(lat(i), i // nb_seq, (i - n_lat_tiles) // ctx_steps)
        return t // n_ctx + b

    kern = functools.partial(_attn_kernel, nb_seq=nb_seq, n_lat_tiles=n_lat_tiles, n_kv=n_kv, group=group)
    kv_spec = lambda f, cb: pl.BlockSpec((TQ, kv_width), lambda i: (f(i), cb))
    cur_spec = lambda cb: pl.BlockSpec((ts, kv_width), lambda i: (i, cb))
    return pl.pallas_call(
        kern,
        grid=(n_rows // ts,),
        in_specs=[pl.BlockSpec(memory_space=pltpu.SMEM),
                  pl.BlockSpec((ts, q_width), lambda i: (i, qcb)),
                  kv_spec(prev_idx, kcb), cur_spec(kcb), kv_spec(next_idx, kcb),
                  kv_spec(prev_idx, vcb), cur_spec(vcb), kv_spec(next_idx, vcb),
                  pl.BlockSpec((n_ctx, kv_width), lambda i: (ctx_idx(i), kcb)),
                  pl.BlockSpec((n_ctx, kv_width), lambda i: (ctx_idx(i), vcb))],
        out_specs=pl.BlockSpec((ts, q_width), lambda i: (i, 0)),
        out_shape=jax.ShapeDtypeStruct((n_rows, q_width), BF16),
        compiler_params=_cparams("parallel"),
        name="window_attention",
    )(sink, proj, proj, proj, proj, proj, proj, proj, proj, proj)


def _merge_kernel(*refs, two_src, tiles_per_seq, n_lat_tiles, alpha):
    if two_src:
        xa_ref, xb_ref = refs[:2]
        refs = refs[2:]
    else:
        xa_ref = refs[0]
        refs = refs[1:]
    (attn_ref, cb_ref, cc_ref, cu_ref, ga_ref, gc_ref, ccp_ref, cup_ref, ccn_ref, cun_ref,
     mod_ref, cw_ref, wa_ref, wc_ref, wo_ref, g_ref, b_ref, o_ref) = refs
    i = pl.program_id(0)
    tm = xa_ref.shape[0]
    hr = ccp_ref.shape[0]
    p = i % tiles_per_seq
    is_lat = i < n_lat_tiles
    first = jnp.where(is_lat, p == 0, True)
    last = jnp.where(is_lat, p == tiles_per_seq - 1, True)
    x = jnp.where(is_lat, xa_ref[...], xb_ref[...]) if two_src else xa_ref[...]

    up = cc_ref[...].astype(F32) * cu_ref[...].astype(F32)
    prev_row = ccp_ref[hr - 1:hr, :].astype(F32) * cup_ref[hr - 1:hr, :].astype(F32)
    next_row = ccn_ref[0:1, :].astype(F32) * cun_ref[0:1, :].astype(F32)
    prev_row = jnp.where(first, 0.0, prev_row)
    next_row = jnp.where(last, 0.0, next_row)
    ridx = lax.broadcasted_iota(jnp.int32, up.shape, 0)
    dn = jnp.where(ridx == 0, prev_row, pltpu.roll(up, 1, 0))
    un = jnp.where(ridx == tm - 1, next_row, pltpu.roll(up, tm - 1, 0))
    conv = cw_ref[0:1, :] * dn + cw_ref[1:2, :] * up + cw_ref[2:3, :] * un
    sconv = (cb_ref[...].astype(F32) * conv).astype(BF16)

    a = jnp.dot(attn_ref[...], wa_ref[...], preferred_element_type=F32)
    s = jnp.dot(sconv, wc_ref[...], preferred_element_type=F32)
    m = (ga_ref[...].astype(F32) * a + gc_ref[...].astype(F32) * s).astype(BF16)
    y = jnp.dot(m, wo_ref[...], preferred_element_type=F32)
    z = alpha * x + mod_ref[0, 2:3, :] * y
    o_ref[...] = _layer_norm(z, g_ref[...], b_ref[...])


def _merge(x_lat, x_ctx, attn, proj, mod, conv_w, wa, wc, wo, g, b, *, n_rows, seg_of_row_tile, d, n_seq, n_ctx,
           n_batch, alpha):
    hr = 16
    assert n_seq % TM == 0 and n_ctx == TM
    tiles_per_seq = n_seq // TM
    n_lat_tiles = n_batch * tiles_per_seq
    two_src = x_ctx is not None
    kern = functools.partial(_merge_kernel, two_src=two_src, tiles_per_seq=tiles_per_seq, n_lat_tiles=n_lat_tiles,
                             alpha=alpha)
    col = lambda cb: pl.BlockSpec((TM, d), lambda i: (i, cb))
    rpt = TM // hr
    n_hblk = proj.shape[0] // hr
    prev = lambda cb: pl.BlockSpec((hr, d), lambda i: (jnp.maximum(i * rpt - 1, 0), cb))
    nxt = lambda cb: pl.BlockSpec((hr, d), lambda i: (jnp.minimum((i + 1) * rpt, n_hblk - 1), cb))
    const = lambda shape: pl.BlockSpec(shape, lambda i: (0,) * len(shape), pipeline_mode=pl.Buffered(1))
    if two_src:
        x_specs = [pl.BlockSpec((TM, d), lambda i: (jnp.minimum(i, n_lat_tiles - 1), 0)),
                   pl.BlockSpec((TM, d), lambda i: (jnp.maximum(i - n_lat_tiles, 0), 0))]
        x_args = (x_lat, x_ctx)
    else:
        x_specs = [pl.BlockSpec((TM, d), lambda i: (i, 0))]
        x_args = (x_lat,)
    return pl.pallas_call(
        kern,
        grid=(n_rows // TM,),
        in_specs=x_specs + [pl.BlockSpec((TM, attn.shape[1]), lambda i: (i, 0)),
                            col(0), col(1), col(2), col(3), col(4),
                            prev(1), prev(2), nxt(1), nxt(2),
                            pl.BlockSpec((1, 6, d), lambda i: (seg_of_row_tile(i, TM), 0, 0)),
                            const((3, d)), const(wa.shape), const(wc.shape), const(wo.shape),
                            const((1, d)), const((1, d))],
        out_specs=pl.BlockSpec((TM, d), lambda i: (i, 0)),
        out_shape=jax.ShapeDtypeStruct((n_rows, d), F32),
        compiler_params=_cparams("parallel"),
        name="merge_ln1",
    )(*x_args, attn, proj, proj, proj, proj, proj, proj, proj, proj, proj, mod, conv_w, wa, wc, wo,
      g.reshape(1, d), b.reshape(1, d))


def _router_kernel(x_ref, mod_ref, rwt_ref, rb_ref, e_ref, w_ref, rank_ref, cnt_ref, cnt_scr, *, n_groups):
    i = pl.program_id(0)
    tm = x_ref.shape[0]
    n_exp = rwt_ref.shape[0]
    per = n_exp // n_groups

    @pl.when(i == 0)
    def _():
        cnt_scr[...] = jnp.zeros_like(cnt_scr)

    h2 = x_ref[...] * (1.0 + mod_ref[0, 4:5, :]) + mod_ref[0, 3:4, :]
    logits = lax.dot_general(rwt_ref[...].astype(BF16), h2.astype(BF16), NT_DIMS,
                             preferred_element_type=F32)
    ex = jnp.exp(logits - jnp.max(logits, axis=0, keepdims=True))
    probs = ex / jnp.sum(ex, axis=0, keepdims=True)
    sel3 = (probs + rb_ref[...]).reshape(n_groups, per, tm)

    li = lax.broadcasted_iota(jnp.int32, (n_groups, per, tm), 1).astype(F32)
    m1 = jnp.max(sel3, axis=1, keepdims=True)
    i1 = jnp.min(jnp.where(sel3 == m1, li, float(per)), axis=1, keepdims=True)
    sel3b = jnp.where(li == i1, -jnp.inf, sel3)
    m2 = jnp.max(sel3b, axis=1, keepdims=True)
    i2 = jnp.min(jnp.where(sel3b == m2, li, float(per)), axis=1, keepdims=True)
    score = m1 + m2
    gi = lax.broadcasted_iota(jnp.int32, (n_groups, 1, tm), 0).astype(F32)
    gbest = jnp.min(jnp.where(score == jnp.max(score, axis=0, keepdims=True), gi, float(n_groups)),
                    axis=0, keepdims=True)
    pick = gi == gbest
    l1 = jnp.sum(jnp.where(pick, i1, 0.0), axis=0)
    l2 = jnp.sum(jnp.where(pick, i2, 0.0), axis=0)
    e1 = gbest[0] * per + l1
    e2 = gbest[0] * per + l2

    ei = lax.broadcasted_iota(jnp.int32, (n_exp, tm), 0).astype(F32)
    is1 = ei == e1
    is2 = ei == e2
    p1 = jnp.sum(jnp.where(is1, probs, 0.0), axis=0, keepdims=True)
    p2 = jnp.sum(jnp.where(is2, probs, 0.0), axis=0, keepdims=True)
    psum = p1 + p2

    onehot = jnp.where(jnp.logical_or(is1, is2), 1.0, 0.0)
    srow = lax.broadcasted_iota(jnp.int32, (tm, tm), 0)
    scol = lax.broadcasted_iota(jnp.int32, (tm, tm), 1)
    before = jnp.where(srow < scol, 1.0, 0.0).astype(BF16)
    excl = jnp.dot(onehot.astype(BF16), before, preferred_element_type=F32) + cnt_scr[:, 0:1]
    r1 = jnp.sum(jnp.where(is1, excl, 0.0), axis=0, keepdims=True)
    r2 = jnp.sum(jnp.where(is2, excl, 0.0), axis=0, keepdims=True)
    cnt_scr[...] = cnt_scr[...] + jnp.sum(onehot, axis=1, keepdims=True)

    e_ref[0:1, :] = e1.astype(jnp.int32)
    e_ref[1:2, :] = e2.astype(jnp.int32)
    w_ref[0:1, :] = p1 / psum
    w_ref[1:2, :] = p2 / psum
    rank_ref[0:1, :] = r1.astype(jnp.int32)
    rank_ref[1:2, :] = r2.astype(jnp.int32)
    cnt_ref[...] = cnt_scr[...]


def _router(x1, mod, rwt, rb, *, seg_of_row_tile, d):
    r = x1.shape[0]
    n_exp = rwt.shape[0]
    row2 = lambda dt: jax.ShapeDtypeStruct((TOP_K, r), dt)
    spec2 = pl.BlockSpec((TOP_K, TM_ROUTER), lambda i: (0, i))
    assert r % TM_ROUTER == 0
    return pl.pallas_call(
        functools.partial(_router_kernel, n_groups=N_EXPERT_GROUPS),
        grid=(r // TM_ROUTER,),
        in_specs=[pl.BlockSpec((TM_ROUTER, d), lambda i: (i, 0)),
                  pl.BlockSpec((1, 6, d), lambda i: (seg_of_row_tile(i, TM_ROUTER), 0, 0)),
                  pl.BlockSpec((n_exp, d), lambda i: (0, 0)),
                  pl.BlockSpec((n_exp, 1), lambda i: (0, 0))],
        out_specs=[spec2, spec2, spec2, pl.BlockSpec((n_exp, LANE), lambda i: (0, 0))],
        out_shape=[row2(jnp.int32), row2(F32), row2(jnp.int32), jax.ShapeDtypeStruct((n_exp, LANE), F32)],
        scratch_shapes=[pltpu.VMEM((n_exp, LANE), F32)],
        compiler_params=_cparams("arbitrary"),
        name="router",
    )(x1, mod, rwt, rb.reshape(n_exp, 1))


def _dispatch_kernel(dest_ref, fill_lo_ref, fill_hi_ref, x_ref, mod_ref, xs_ref, buf, zslab, sem, zsem, *, n_rows):
    i = pl.program_id(0)
    n_steps = pl.num_programs(0)
    tm = x_ref.shape[0]
    words = _pack_rows(x_ref[...] * (1.0 + mod_ref[0, 4:5, :]) + mod_ref[0, 3:4, :])

    def drain():
        for _ in range(TOP_K):
            pltpu.make_async_copy(buf, xs_ref.at[pl.ds(0, tm * SUBLANE)], sem).wait()

    @pl.when(i > 0)
    def _():
        drain()

    for s, w in enumerate(words):
        buf[pl.ds(s, tm, stride=SUBLANE), :] = w

    def issue(t, carry):
        src = buf.at[pl.ds(pl.multiple_of(t * SUBLANE, SUBLANE), SUBLANE)]
        for k in range(TOP_K):
            dst = dest_ref[k * n_rows + i * tm + t]
            pltpu.make_async_copy(src, xs_ref.at[pl.ds(pl.multiple_of(dst * SUBLANE, SUBLANE), SUBLANE)],
                                  sem).start(priority=k % 2)
        return carry

    lax.fori_loop(0, tm, issue, 0, unroll=ISSUE_UNROLL)

    @pl.when(i == n_steps - 1)
    def _():
        drain()
        zslab[...] = jnp.zeros_like(zslab)
        n_ranges = fill_lo_ref.shape[0]

        def zero_copy(rr):
            return pltpu.make_async_copy(zslab, xs_ref.at[pl.ds(pl.multiple_of(rr * SUBLANE, SUBLANE), SUBLANE)], zsem)

        def per_range(fn):
            def body(e, carry):
                lax.fori_loop(fill_lo_ref[e], fill_hi_ref[e], lambda rr, c: (fn(rr), c)[1], 0)
                return carry
            lax.fori_loop(0, n_ranges, body, 0)

        per_range(lambda rr: zero_copy(rr).start())
        per_range(lambda rr: zero_copy(rr).wait())


def _dispatch(x1, mod, dest_flat, fill_lo, fill_hi, cap, *, seg_of_row_tile, d):
    r = x1.shape[0]
    slab = _slab_rows(d)
    grid_spec = pltpu.PrefetchScalarGridSpec(
        num_scalar_prefetch=3,
        grid=(r // TM,),
        in_specs=[pl.BlockSpec((TM, d), lambda i, *_: (i, 0)),
                  pl.BlockSpec((1, 6, d), lambda i, *_: (seg_of_row_tile(i, TM), 0, 0))],
        out_specs=pl.BlockSpec(memory_space=pl.ANY),
        scratch_shapes=[pltpu.VMEM((TM * slab, LANE), U32), pltpu.VMEM((slab, LANE), U32),
                        pltpu.SemaphoreType.DMA(()), pltpu.SemaphoreType.DMA(())],
    )
    return pl.pallas_call(
        functools.partial(_dispatch_kernel, n_rows=r),
        grid_spec=grid_spec,
        out_shape=jax.ShapeDtypeStruct((cap * slab, LANE), U32),
        compiler_params=_cparams("arbitrary"),
        name="moe_dispatch",
    )(dest_flat, fill_lo, fill_hi, x1, mod)


def _expert_kernel(blk_e_ref, n_used_ref, xs_ref, wg_ref, wu_ref, wd_ref, o_ref):
    b = pl.program_id(0)
    mb = xs_ref.shape[0] // SUBLANE

    @pl.when(b < n_used_ref[0])
    def _():
        xb = _unpack_rows([xs_ref[pl.ds(s, mb, stride=SUBLANE), :] for s in range(SUBLANE)]).astype(BF16)
        g = jnp.dot(xb, wg_ref[...], preferred_element_type=F32)
        u = jnp.dot(xb, wu_ref[...], preferred_element_type=F32)
        hmid = (g * jax.nn.sigmoid(g) * u).astype(BF16)
        y = jnp.dot(hmid, wd_ref[...], preferred_element_type=F32)
        for s, w in enumerate(_pack_rows(y)):
            o_ref[pl.ds(s, mb, stride=SUBLANE), :] = w

    @pl.when(b >= n_used_ref[0])
    def _():
        o_ref[...] = jnp.zeros_like(o_ref)


def _experts(xs, blk_expert, n_used, wg, wu, wd):
    d, ff = wg.shape[1], wg.shape[2]
    slab = _slab_rows(d)
    n_blk = xs.shape[0] // (MOE_BLOCK * slab)
    xrow = lambda b, be, nu: (jnp.minimum(b, nu[0] - 1), 0)
    grid_spec = pltpu.PrefetchScalarGridSpec(
        num_scalar_prefetch=2,
        grid=(n_blk,),
        in_specs=[pl.BlockSpec((MOE_BLOCK * slab, LANE), xrow),
                  pl.BlockSpec((None, d, ff), lambda b, be, nu: (be[b], 0, 0)),
                  pl.BlockSpec((None, d, ff), lambda b, be, nu: (be[b], 0, 0)),
                  pl.BlockSpec((None, ff, d), lambda b, be, nu: (be[b], 0, 0))],
        out_specs=pl.BlockSpec((MOE_BLOCK * slab, LANE), lambda b, be, nu: (b, 0)),
    )
    return pl.pallas_call(
        _expert_kernel,
        grid_spec=grid_spec,
        out_shape=jax.ShapeDtypeStruct(xs.shape, U32),
        compiler_params=_cparams("arbitrary"),
        name="moe_experts",
    )(blk_expert, n_used, xs, wg, wu, wd)


def _combine_kernel(*refs, n_rows, alpha, emit_h):
    bufs, sems = refs[-COMBINE_SLOTS - 1:-1], refs[-1]
    refs = refs[:-COMBINE_SLOTS - 1]
    if emit_h:
        dest_ref, x_ref, wt_ref, mod_ref, g_ref, b_ref, modn_ref, ys_ref, o_ref, h_ref = refs
    else:
        dest_ref, x_ref, wt_ref, mod_ref, g_ref, b_ref, ys_ref, o_ref = refs
    i = pl.program_id(0)
    n_steps = pl.num_programs(0)
    tm = x_ref.shape[0]
    n_slots = len(bufs)
    ahead = n_slots - 1
    slot = i % n_slots

    def gather(tile, sl, t, k):
        src = dest_ref[k * n_rows + tile * tm + t]
        row0 = t * SUBLANE if isinstance(t, int) else pl.multiple_of(t * SUBLANE, SUBLANE)
        return pltpu.make_async_copy(ys_ref.at[pl.ds(pl.multiple_of(src * SUBLANE, SUBLANE), SUBLANE)],
                                     bufs[sl].at[k, pl.ds(row0, SUBLANE)], sems.at[sl])

    def wait_tile(sl):
        for k in range(TOP_K):
            pltpu.make_async_copy(ys_ref.at[pl.ds(0, tm * SUBLANE)], bufs[sl].at[k], sems.at[sl]).wait()

    @pl.when(i == 0)
    def _():
        for j in range(ahead):
            def body(t, carry, j=j):
                for k in range(TOP_K):
                    gather(jnp.minimum(j, n_steps - 1), j, t, k).start(priority=k % 2)
                return carry
            lax.fori_loop(0, tm, body, 0, unroll=ISSUE_UNROLL)

    def finish(sl):
        wait_tile(sl)
        nxt = jnp.minimum(i + ahead, n_steps - 1)
        for t in range(tm):
            for k in range(TOP_K):
                gather(nxt, (sl + ahead) % n_slots, t, k).start(priority=k % 2)
        y = [_unpack_rows([bufs[sl][k, pl.ds(s, tm, stride=SUBLANE), :] for s in range(SUBLANE)])
             for k in range(TOP_K)]
        f = wt_ref[:, 0:1] * y[0] + wt_ref[:, 1:2] * y[1]
        z = alpha * x_ref[...] + mod_ref[0, 5:6, :] * f
        x2 = _layer_norm(z, g_ref[...], b_ref[...])
        o_ref[...] = x2
        if emit_h:
            h_ref[...] = (x2 * (1.0 + modn_ref[0, 1:2, :]) + modn_ref[0, 0:1, :]).astype(BF16)

    for sl in range(n_slots):
        pl.when(slot == sl)(functools.partial(finish, sl))

    for sl in range(n_slots):
        pl.when(jnp.logical_and(i == n_steps - 1, slot != sl))(functools.partial(wait_tile, sl))


def _combine(x1, wt_rows, mod, g, b, dest_flat, ys, mod_next, *, seg_of_row_tile, d, alpha):
    r = x1.shape[0]
    slab = _slab_rows(d)
    emit_h = mod_next is not None
    mod_spec = pl.BlockSpec((1, 6, d), lambda i, *_: (seg_of_row_tile(i, TM), 0, 0))
    row_spec = pl.BlockSpec((TM, d), lambda i, *_: (i, 0))
    vec_spec = pl.BlockSpec((1, d), lambda i, *_: (0, 0))
    in_specs = [row_spec, pl.BlockSpec((TM, TOP_K), lambda i, *_: (i, 0)), mod_spec, vec_spec, vec_spec]
    args = [x1, wt_rows, mod, g.reshape(1, d), b.reshape(1, d)]
    out_specs, out_shape = [row_spec], [jax.ShapeDtypeStruct((r, d), F32)]
    if emit_h:
        in_specs.append(mod_spec)
        args.append(mod_next)
        out_specs.append(row_spec)
        out_shape.append(jax.ShapeDtypeStruct((r, d), BF16))
    in_specs.append(pl.BlockSpec(memory_space=pl.ANY))
    args.append(ys)
    grid_spec = pltpu.PrefetchScalarGridSpec(
        num_scalar_prefetch=1,
        grid=(r // TM,),
        in_specs=in_specs,
        out_specs=out_specs,
        scratch_shapes=[pltpu.VMEM((TOP_K, TM * slab, LANE), U32) for _ in range(COMBINE_SLOTS)]
        + [pltpu.SemaphoreType.DMA((COMBINE_SLOTS,))],
    )
    outs = pl.pallas_call(
        functools.partial(_combine_kernel, n_rows=r, alpha=alpha, emit_h=emit_h),
        grid_spec=grid_spec,
        out_shape=out_shape,
        compiler_params=_cparams("arbitrary"),
        name="moe_combine_ln2",
    )(dest_flat, *args)
    return (outs[0], outs[1]) if emit_h else (outs[0], None)


def _rope_tables(n_batch, n_seq, ctx_rows):
    nf = HEAD_DIM // 4
    inv_freq = np.power(np.float32(ROPE_BASE), -np.arange(nf, dtype=np.float32) / np.float32(nf)).astype(np.float32)
    n_grid_rows = n_seq // GRID_W
    ar = (np.arange(n_grid_rows, dtype=np.float32)[:, None] * inv_freq).astype(np.float32)
    ac = (np.arange(GRID_W, dtype=np.float32)[:, None] * inv_freq).astype(np.float32)
    shape = (n_grid_rows, GRID_W, nf)
    by_row = lambda a: jnp.broadcast_to(jnp.asarray(a, F32)[:, None, :], shape)
    by_col = lambda a: jnp.broadcast_to(jnp.asarray(a, F32)[None, :, :], shape)
    cr, sr, cc, sc = np.cos(ar), np.sin(ar), np.cos(ac), np.sin(ac)
    cos_t = jnp.concatenate([by_row(cr), by_row(cr), by_col(cc), by_col(cc)], axis=-1).reshape(n_seq, HEAD_DIM)
    sin_t = jnp.concatenate([by_row(-sr), by_row(sr), by_col(-sc), by_col(sc)], axis=-1).reshape(n_seq, HEAD_DIM)
    cos_t = jnp.concatenate([cos_t] * n_batch + [jnp.ones((ctx_rows, HEAD_DIM), F32)], axis=0)
    sin_t = jnp.concatenate([sin_t] * n_batch + [jnp.zeros((ctx_rows, HEAD_DIM), F32)], axis=0)
    return cos_t, sin_t


def kernel(x, c, ctx, c_ctx, w_ada, b_ada, w_in, attn_sink, conv_w, w_attn_proj, w_conv_proj, w_out,
           ln1_g, ln1_b, ln2_g, ln2_b, router_w, router_b, w_gate, w_up, w_down):
    n_batch, n_seq, d = x.shape
    n_ctx = ctx.shape[1]
    depth = w_ada.shape[0]
    n_exp = router_w.shape[1]
    q_width = attn_sink.shape[1] * HEAD_DIM
    in_width = w_in.shape[2]
    kv_width = (in_width - q_width - 5 * d) // 2
    t = n_batch * n_seq
    r = t + n_batch * n_ctx
    alpha = float((2 * depth) ** 0.25)
    assert w_conv_proj.shape[1] == d and n_seq % GRID_W == 0
    assert r % TM_IN == 0 and r % TM == 0 and r % TQ == 0

    def seg_of_row_tile(i, tile):
        return jnp.minimum((i * tile) // n_seq, n_batch)

    x_lat, x_ctx = x.reshape(t, d), ctx.reshape(n_batch * n_ctx, d)
    rope_c, rope_s = _rope_tables(n_batch, n_seq, n_batch * n_ctx)
    c_rows = jnp.concatenate([c, c_ctx[None, :], jnp.zeros((8 - n_batch - 1, d), F32)], axis=0)
    mods = [_mod_table(c_rows, w_ada, b_ada[l], l).reshape(8, 6, d) for l in range(depth)]
    rwt = router_w.T
    q_col, k_col, v_col = 5 * d, 5 * d + q_width, 5 * d + q_width + kv_width
    expert_ids = jnp.arange(n_exp, dtype=jnp.int32)

    h = _modulate(x_lat, x_ctx, mods[0], seg_of_row_tile=seg_of_row_tile)
    x_cur = None
    for l in range(depth):
        last = l == depth - 1
        mod = mods[l]
        n_rows = t if last else r
        proj, wg_b, wu_b, wd_b = _in_proj(h, w_in, l, rope_c, rope_s, w_gate, w_up, w_down, d=d, q_width=q_width,
                                          kv_width=kv_width)
        attn = _attention(proj, attn_sink[l], n_rows=n_rows, n_batch=n_batch, n_seq=n_seq, n_ctx=n_ctx, q_col=q_col,
                          k_col=k_col, v_col=v_col, q_width=q_width, kv_width=kv_width)
        src = (x_lat, x_ctx) if l == 0 else (x_cur, None)
        x1 = _merge(src[0], src[1], attn, proj, mod, conv_w[l], w_attn_proj[l].astype(BF16),
                    w_conv_proj[l].astype(BF16), w_out[l].astype(BF16), ln1_g[l], ln1_b[l], n_rows=n_rows,
                    seg_of_row_tile=seg_of_row_tile, d=d, n_seq=n_seq, n_ctx=n_ctx, n_batch=n_batch, alpha=alpha)

        e_idx, wts, rank, cnt = _router(x1, mod, rwt, router_b, seg_of_row_tile=seg_of_row_tile, d=d)
        cap = (n_rows * TOP_K + n_exp * (MOE_BLOCK - 1) + MOE_BLOCK - 1) // MOE_BLOCK * MOE_BLOCK
        n_blk = cap // MOE_BLOCK
        counts = cnt[:, 0].astype(jnp.int32)
        padded = (counts + MOE_BLOCK - 1) // MOE_BLOCK * MOE_BLOCK
        pend = jnp.cumsum(padded)
        pstart = pend - padded
        slot0 = jnp.sum(jnp.where(e_idx[:, :, None] == expert_ids, pstart, 0), axis=-1)
        dest_flat = (slot0 + rank).reshape(-1)
        n_used = (pend[-1] // MOE_BLOCK).astype(jnp.int32)
        blk_start = jnp.minimum(jnp.arange(n_blk, dtype=jnp.int32), n_used - 1) * MOE_BLOCK
        blk_expert = jnp.sum(blk_start[:, None] >= pend[None, :], axis=-1).astype(jnp.int32)
        fill_lo = jnp.concatenate([pstart + counts, pend[-1:]]).astype(jnp.int32)
        fill_hi = jnp.concatenate([pend, jnp.full((1,), cap, pend.dtype)]).astype(jnp.int32)
        xs = _dispatch(x1, mod, dest_flat, fill_lo, fill_hi, cap, seg_of_row_tile=seg_of_row_tile, d=d)
        ys = _experts(xs, blk_expert, n_used.reshape(1), wg_b, wu_b, wd_b)
        x_cur, h = _combine(x1, wts.T, mod, ln2_g[l], ln2_b[l], dest_flat, ys, None if last else mods[l + 1],
                            seg_of_row_tile=seg_of_row_tile, d=d, alpha=alpha)

    return x_cur.reshape(n_batch, n_seq, d)
```

```python
import functools

import numpy as np
import jax
import jax.numpy as jnp
from jax import lax
from jax.experimental import pallas as pl
from jax.experimental.pallas import tpu as pltpu

F32 = jnp.float32
BF16 = jnp.bfloat16
U32 = jnp.uint32
HIGHEST = lax.Precision.HIGHEST

GRID_W = 64
HEAD_DIM = 128
WINDOW = 128
ROPE_BASE = 10000.0
N_EXPERT_GROUPS = 4
TOP_K = 2
LN_EPS = 1e-5
NEG_INF = -1e30

LANE = 128
SUBLANE = 8
TQ = WINDOW
ATTN_TILES = 2
TM = 256
TM_IN = 1280
TN_IN = 1024
TN_CHUNK = 256
EXPERT_CAST_BLOCKS = 256
TM_ROUTER = 512
TN_MOD = 1024
MOE_BLOCK = 256
ISSUE_UNROLL = 8
COMBINE_SLOTS = 3
VMEM_LIMIT = 56 * 1024 * 1024

NT_DIMS = (((1,), (1,)), ((), ()))
TN_DIMS = (((0,), (0,)), ((), ()))
LOG2E = 1.4426950408889634


def _cparams(*sem):
    return pltpu.CompilerParams(dimension_semantics=sem, vmem_limit_bytes=VMEM_LIMIT)


def _layer_norm(z, g, b):
    mu = jnp.mean(z, axis=-1, keepdims=True)
    zc = z - mu
    var = jnp.mean(zc * zc, axis=-1, keepdims=True)
    return zc * lax.rsqrt(var + LN_EPS) * g + b


def _pack_rows(v):
    half = v.shape[1] // 2
    bits = lax.bitcast_convert_type(v.astype(BF16).astype(F32), U32)
    words = []
    for s in range(half // LANE):
        lo = bits[:, s * LANE:(s + 1) * LANE] >> 16
        hi = bits[:, half + s * LANE:half + (s + 1) * LANE] & np.uint32(0xFFFF0000)
        words.append(lo | hi)
    return words


def _unpack_rows(words):
    lo = [lax.bitcast_convert_type(w << 16, F32) for w in words]
    hi = [lax.bitcast_convert_type(w & np.uint32(0xFFFF0000), F32) for w in words]
    return jnp.concatenate(lo + hi, axis=1)


def _slab_rows(d):
    assert d % (2 * LANE) == 0 and d // (2 * LANE) == SUBLANE, "one row must pack into one (8, 128) uint32 tile"
    return d // (2 * LANE)


def _mod_kernel(c_ref, w_ref, b_ref, o_ref):
    cc = c_ref[...]
    s = cc * jax.nn.sigmoid(cc)
    o_ref[...] = jnp.dot(s, w_ref[...], preferred_element_type=F32, precision=HIGHEST) + b_ref[...]


def _mod_table(c_rows, w_ada, b_ada_l, layer):
    rows, d = c_rows.shape
    n_out = w_ada.shape[2]
    return pl.pallas_call(
        _mod_kernel,
        grid=(n_out // TN_MOD,),
        in_specs=[pl.BlockSpec((rows, d), lambda j: (0, 0)),
                  pl.BlockSpec((None, d, TN_MOD), lambda j: (layer, 0, j)),
                  pl.BlockSpec((1, TN_MOD), lambda j: (0, j))],
        out_specs=pl.BlockSpec((rows, TN_MOD), lambda j: (0, j)),
        out_shape=jax.ShapeDtypeStruct((rows, n_out), F32),
        compiler_params=_cparams("parallel"),
        name="mod_table",
    )(c_rows, w_ada, b_ada_l.reshape(1, n_out))


def _modulate_kernel(xa_ref, xb_ref, mod_ref, o_ref, *, n_lat_tiles):
    x = jnp.where(pl.program_id(0) < n_lat_tiles, xa_ref[...], xb_ref[...])
    o_ref[...] = (x * (1.0 + mod_ref[0, 1:2, :]) + mod_ref[0, 0:1, :]).astype(BF16)


def _modulate(x_lat, x_ctx, mod, *, seg_of_row_tile):
    t, d = x_lat.shape
    r = t + x_ctx.shape[0]
    n_lat = t // TM
    return pl.pallas_call(
        functools.partial(_modulate_kernel, n_lat_tiles=n_lat),
        grid=(r // TM,),
        in_specs=[pl.BlockSpec((TM, d), lambda i: (jnp.minimum(i, n_lat - 1), 0)),
                  pl.BlockSpec((TM, d), lambda i: (jnp.maximum(i - n_lat, 0), 0)),
                  pl.BlockSpec((1, 6, d), lambda i: (seg_of_row_tile(i, TM), 0, 0))],
        out_specs=pl.BlockSpec((TM, d), lambda i: (i, 0)),
        out_shape=jax.ShapeDtypeStruct((r, d), BF16),
        compiler_params=_cparams("parallel"),
        name="modulate_in",
    )(x_lat, x_ctx, mod)


def _inproj_kernel(h_ref, w_ref, rc_ref, rs_ref, eg_ref, eu_ref, ed_ref, o_ref, ego_ref, euo_ref, edo_ref, wb_scr, *,
                   n_plain, n_gate, n_q, k_cols, scale):
    j = pl.program_id(0)
    tn = o_ref.shape[1]

    @pl.when(pl.program_id(1) == 0)
    def _():
        wb_scr[...] = w_ref[...].astype(BF16)

    def rope(xh):
        lane = lax.broadcasted_iota(jnp.int32, xh.shape, 1)
        sw = jnp.where((lane & 32) == 0, pltpu.roll(xh, LANE - 32, 1), pltpu.roll(xh, 32, 1))
        return xh * rc_ref[...] + sw * rs_ref[...]

    def column_chunks(epilogue):
        ego_ref[...] = eg_ref[...].astype(BF16)
        euo_ref[...] = eu_ref[...].astype(BF16)
        edo_ref[...] = ed_ref[...].astype(BF16)
        for c0 in range(0, tn, TN_CHUNK):
            acc = jnp.dot(h_ref[...], wb_scr[:, c0:c0 + TN_CHUNK], preferred_element_type=F32)
            for c in range(c0, c0 + TN_CHUNK, LANE):
                o_ref[:, c:c + LANE] = epilogue(acc[:, c - c0:c - c0 + LANE], c).astype(BF16)

    @pl.when(j < n_plain)
    def _():
        column_chunks(lambda a, c: a)

    @pl.when((j >= n_plain) & (j < n_plain + n_gate))
    def _():
        column_chunks(lambda a, c: jax.nn.sigmoid(a))

    @pl.when((j >= n_plain + n_gate) & (j < n_plain + n_gate + n_q))
    def _():
        column_chunks(lambda a, c: rope(a) * scale)

    @pl.when(j == n_plain + n_gate + n_q)
    def _():
        column_chunks(lambda a, c: rope(a) if c < k_cols else a)


def _in_proj(h, w_in, layer, rope_c, rope_s, w_gate, w_up, w_down, *, d, q_width, kv_width):
    r = h.shape[0]
    in_width = w_in.shape[2]
    base = q_width + 2 * kv_width
    assert (3 * d) % TN_IN == 0 and (2 * d) % TN_IN == 0 and q_width % TN_IN == 0 and 2 * kv_width == TN_IN
    n_plain, n_gate, n_q = 3 * d // TN_IN, 2 * d // TN_IN, q_width // TN_IN
    n_j = in_width // TN_IN
    assert n_plain + n_gate + n_q + 1 == n_j and base % TN_IN == 0 and r % TM_IN == 0
    shift = base // TN_IN

    kern = functools.partial(_inproj_kernel, n_plain=n_plain, n_gate=n_gate, n_q=n_q, k_cols=kv_width,
                             scale=HEAD_DIM ** -0.5 * LOG2E)
    n_i = r // TM_IN
    depth, n_exp, _, ff = w_gate.shape
    assert n_j * n_i >= EXPERT_CAST_BLOCKS
    rows_gu, rows_d = n_exp * d // EXPERT_CAST_BLOCKS, n_exp * ff // EXPERT_CAST_BLOCKS
    assert rows_gu * EXPERT_CAST_BLOCKS == n_exp * d and rows_d * EXPERT_CAST_BLOCKS == n_exp * ff
    assert rows_gu % 16 == 0 and rows_d % 16 == 0

    def cast_blk(j, i):
        return jnp.minimum(j * n_i + i, EXPERT_CAST_BLOCKS - 1)

    cast_in = lambda rows, cols: pl.BlockSpec((None, rows, cols), lambda j, i: (layer, cast_blk(j, i), 0))
    cast_out = lambda rows, cols: pl.BlockSpec((rows, cols), lambda j, i: (cast_blk(j, i), 0))
    proj, wg_b, wu_b, wd_b = pl.pallas_call(
        kern,
        grid=(n_j, n_i),
        in_specs=[pl.BlockSpec((TM_IN, d), lambda j, i: (i, 0)),
                  pl.BlockSpec((None, d, TN_IN), lambda j, i: (layer, 0, (j + shift) % n_j)),
                  pl.BlockSpec((TM_IN, LANE), lambda j, i: (i, 0)),
                  pl.BlockSpec((TM_IN, LANE), lambda j, i: (i, 0)),
                  cast_in(rows_gu, ff), cast_in(rows_gu, ff), cast_in(rows_d, d)],
        out_specs=[pl.BlockSpec((TM_IN, TN_IN), lambda j, i: (i, j)),
                   cast_out(rows_gu, ff), cast_out(rows_gu, ff), cast_out(rows_d, d)],
        out_shape=[jax.ShapeDtypeStruct((r, in_width), BF16),
                   jax.ShapeDtypeStruct((n_exp * d, ff), BF16), jax.ShapeDtypeStruct((n_exp * d, ff), BF16),
                   jax.ShapeDtypeStruct((n_exp * ff, d), BF16)],
        scratch_shapes=[pltpu.VMEM((d, TN_IN), BF16)],
        compiler_params=_cparams("arbitrary", "arbitrary"),
        name="in_proj",
    )(h, w_in, rope_c, rope_s, w_gate.reshape(depth, n_exp * d, ff), w_up.reshape(depth, n_exp * d, ff),
      w_down.reshape(depth, n_exp * ff, d))
    return proj, wg_b.reshape(n_exp, d, ff), wu_b.reshape(n_exp, d, ff), wd_b.reshape(n_exp, ff, d)


def _attn_kernel(sink_ref, q_ref, kp_ref, kc_ref, kn_ref, vp_ref, vc_ref, vn_ref, kx_ref, vx_ref, o_ref, *,
                 nb_seq, n_lat_tiles, n_kv, group):
    i = pl.program_id(0)
    tq = kp_ref.shape[0]
    is_lat = i < n_lat_tiles
    p = i % nb_seq
    n_ctx = kx_ref.shape[0]
    key = lax.broadcasted_iota(jnp.int32, (tq, tq), 0)
    qry = lax.broadcasted_iota(jnp.int32, (tq, tq), 1)
    qrow = lax.broadcasted_iota(jnp.int32, (group * tq, tq), 0)
    qcol = lax.broadcasted_iota(jnp.int32, (group * tq, tq), 1)
    row_onehot = jnp.where(qrow % tq == qcol, 1.0, 0.0).astype(BF16)
    n_sub = kc_ref.shape[0] // tq
    for sub in range(n_sub):
        _attn_tile(sink_ref, q_ref, o_ref, kx_ref, vx_ref, sub * tq, key, qry, row_onehot, is_lat,
                   has_prev=jnp.logical_and(is_lat, p > 0) if sub == 0 else is_lat,
                   has_next=jnp.logical_and(is_lat, p < nb_seq - 1) if sub == n_sub - 1 else is_lat,
                   k_prev=kp_ref if sub == 0 else kc_ref.at[(sub - 1) * tq:sub * tq],
                   k_cur=kc_ref.at[sub * tq:(sub + 1) * tq],
                   k_next=kn_ref if sub == n_sub - 1 else kc_ref.at[(sub + 1) * tq:(sub + 2) * tq],
                   v_prev=vp_ref if sub == 0 else vc_ref.at[(sub - 1) * tq:sub * tq],
                   v_cur=vc_ref.at[sub * tq:(sub + 1) * tq],
                   v_next=vn_ref if sub == n_sub - 1 else vc_ref.at[(sub + 1) * tq:(sub + 2) * tq],
                   n_kv=n_kv, group=group, n_ctx=n_ctx)


def _attn_tile(sink_ref, q_ref, o_ref, kx_ref, vx_ref, r0, key, qry, row_onehot, is_lat, *, has_prev, has_next,
               k_prev, k_cur, k_next, v_prev, v_cur, v_next, n_kv, group, n_ctx):
    tq = key.shape[0]
    b_prev = jnp.where(jnp.logical_and(key >= qry, has_prev), 0.0, NEG_INF)
    b_cur = jnp.where(jnp.logical_and(key >= 0, is_lat), 0.0, NEG_INF)
    b_next = jnp.where(jnp.logical_and(key <= qry, has_next), 0.0, NEG_INF)
    bias_t = jnp.concatenate([b_prev, b_cur, b_next, jnp.zeros((n_ctx, tq), F32)], axis=0).astype(BF16)
    rows = slice(r0, r0 + tq)

    for h in range(n_kv):
        hs = slice(h * HEAD_DIM, (h + 1) * HEAD_DIM)
        q4 = jnp.concatenate([q_ref[rows, (h * group + g) * HEAD_DIM:(h * group + g + 1) * HEAD_DIM]
                              for g in range(group)], axis=0)
        k_all = jnp.concatenate([k_prev[:, hs], k_cur[:, hs], k_next[:, hs], kx_ref[:, hs]], axis=0)
        v_all = jnp.concatenate([v_prev[:, hs], v_cur[:, hs], v_next[:, hs], vx_ref[:, hs]], axis=0)
        s_t = lax.dot_general(jnp.concatenate([k_all, bias_t], axis=1), jnp.concatenate([q4, row_onehot], axis=1),
                              NT_DIMS, preferred_element_type=F32)
        p_cols, inv_cols = [], []
        for g in range(group):
            sg = s_t[:, g * tq:(g + 1) * tq]
            sink2 = sink_ref[h * group + g] * LOG2E
            m = jnp.maximum(jnp.max(sg, axis=0, keepdims=True), sink2)
            p = jnp.exp2(sg - m)
            denom = jnp.sum(p, axis=0, keepdims=True) + jnp.exp2(sink2 - m)
            p_cols.append(p.astype(BF16))
            inv_cols.append(1.0 / denom)
        o_t = lax.dot_general(v_all, jnp.concatenate(p_cols, axis=1), TN_DIMS, preferred_element_type=F32)
        o_t = o_t * jnp.concatenate(inv_cols, axis=1)
        for g in range(group):
            hq = h * group + g
            o_ref[rows, hq * HEAD_DIM:(hq + 1) * HEAD_DIM] = o_t[:, g * tq:(g + 1) * tq].T.astype(BF16)


def _attention(proj, sink, *, n_rows, n_batch, n_seq, n_ctx, q_col, k_col, v_col, q_width, kv_width):
    t = n_batch * n_seq
    ts = ATTN_TILES * TQ
    nb_seq = n_seq // ts
    n_lat_tiles = t // ts
    n_kv = kv_width // HEAD_DIM
    group = q_width // kv_width
    assert q_col % q_width == 0 and k_col % kv_width == 0 and v_col % kv_width == 0
    assert n_seq % ts == 0 and n_ctx % ts == 0 and n_rows % ts == 0 and t % n_ctx == 0
    ctx_steps = n_ctx // ts
    kcb, vcb, qcb = k_col // kv_width, v_col // kv_width, q_col // q_width

    def lat(i):
        return i < n_lat_tiles

    def prev_idx(i):
        return jnp.where(jnp.logical_and(lat(i), i % nb_seq > 0), i * ATTN_TILES - 1, i * ATTN_TILES)

    def next_idx(i):
        return jnp.where(jnp.logical_and(lat(i), i % nb_seq < nb_seq - 1), (i + 1) * ATTN_TILES, i * ATTN_TILES)

    def ctx_idx(i):
        b = jnp.where(lat(i), i // nb_seq, (i - n_lat_tiles) // ctx_steps)
        return t // n_ctx + b

    kern = functools.partial(_attn_kernel, nb_seq=nb_seq, n_lat_tiles=n_lat_tiles, n_kv=n_kv, group=group)
    kv_spec = lambda f, cb: pl.BlockSpec((TQ, kv_width), lambda i: (f(i), cb))
    cur_spec = lambda cb: pl.BlockSpec((ts, kv_width), lambda i: (i, cb))
    return pl.pallas_call(
        kern,
        grid=(n_rows // ts,),
        in_specs=[pl.BlockSpec(memory_space=pltpu.SMEM),
                  pl.BlockSpec((ts, q_width), lambda i: (i, qcb)),
                  kv_spec(prev_idx, kcb), cur_spec(kcb), kv_spec(next_idx, kcb),
                  kv_spec(prev_idx, vcb), cur_spec(vcb), kv_spec(next_idx, vcb),
                  pl.BlockSpec((n_ctx, kv_width), lambda i: (ctx_idx(i), kcb)),
                  pl.BlockSpec((n_ctx, kv_width), lambda i: (ctx_idx(i), vcb))],
        out_specs=pl.BlockSpec((ts, q_width), lambda i: (i, 0)),
        out_shape=jax.ShapeDtypeStruct((n_rows, q_width), BF16),
        compiler_params=_cparams("parallel"),
        name="window_attention",
    )(sink, proj, proj, proj, proj, proj, proj, proj, proj, proj)


def _merge_kernel(*refs, two_src, tiles_per_seq, n_lat_tiles, alpha):
    if two_src:
        xa_ref, xb_ref = refs[:2]
        refs = refs[2:]
    else:
        xa_ref = refs[0]
        refs = refs[1:]
    (attn_ref, cb_ref, cc_ref, cu_ref, ga_ref, gc_ref, ccp_ref, cup_ref, ccn_ref, cun_ref,
     mod_ref, cw_ref, wa_ref, wc_ref, wo_ref, g_ref, b_ref, o_ref) = refs
    i = pl.program_id(0)
    tm = xa_ref.shape[0]
    hr = ccp_ref.shape[0]
    p = i % tiles_per_seq
    is_lat = i < n_lat_tiles
    first = jnp.where(is_lat, p == 0, True)
    last = jnp.where(is_lat, p == tiles_per_seq - 1, True)
    x = jnp.where(is_lat, xa_ref[...], xb_ref[...]) if two_src else xa_ref[...]

    up = cc_ref[...].astype(F32) * cu_ref[...].astype(F32)
    prev_row = ccp_ref[hr - 1:hr, :].astype(F32) * cup_ref[hr - 1:hr, :].astype(F32)
    next_row = ccn_ref[0:1, :].astype(F32) * cun_ref[0:1, :].astype(F32)
    prev_row = jnp.where(first, 0.0, prev_row)
    next_row = jnp.where(last, 0.0, next_row)
    ridx = lax.broadcasted_iota(jnp.int32, up.shape, 0)
    dn = jnp.where(ridx == 0, prev_row, pltpu.roll(up, 1, 0))
    un = jnp.where(ridx == tm - 1, next_row, pltpu.roll(up, tm - 1, 0))
    conv = cw_ref[0:1, :] * dn + cw_ref[1:2, :] * up + cw_ref[2:3, :] * un
    sconv = (cb_ref[...].astype(F32) * conv).astype(BF16)

    a = jnp.dot(attn_ref[...], wa_ref[...], preferred_element_type=F32)
    s = jnp.dot(sconv, wc_ref[...], preferred_element_type=F32)
    m = (ga_ref[...].astype(F32) * a + gc_ref[...].astype(F32) * s).astype(BF16)
    y = jnp.dot(m, wo_ref[...], preferred_element_type=F32)
    z = alpha * x + mod_ref[0, 2:3, :] * y
    o_ref[...] = _layer_norm(z, g_ref[...], b_ref[...])


def _merge(x_lat, x_ctx, attn, proj, mod, conv_w, wa, wc, wo, g, b, *, n_rows, seg_of_row_tile, d, n_seq, n_ctx,
           n_batch, alpha):
    hr = 16
    assert n_seq % TM == 0 and n_ctx == TM
    tiles_per_seq = n_seq // TM
    n_lat_tiles = n_batch * tiles_per_seq
    two_src = x_ctx is not None
    kern = functools.partial(_merge_kernel, two_src=two_src, tiles_per_seq=tiles_per_seq, n_lat_tiles=n_lat_tiles,
                             alpha=alpha)
    col = lambda cb: pl.BlockSpec((TM, d), lambda i: (i, cb))
    rpt = TM // hr
    n_hblk = proj.shape[0] // hr
    prev = lambda cb: pl.BlockSpec((hr, d), lambda i: (jnp.maximum(i * rpt - 1, 0), cb))
    nxt = lambda cb: pl.BlockSpec((hr, d), lambda i: (jnp.minimum((i + 1) * rpt, n_hblk - 1), cb))
    const = lambda shape: pl.BlockSpec(shape, lambda i: (0,) * len(shape), pipeline_mode=pl.Buffered(1))
    if two_src:
        x_specs = [pl.BlockSpec((TM, d), lambda i: (jnp.minimum(i, n_lat_tiles - 1), 0)),
                   pl.BlockSpec((TM, d), lambda i: (jnp.maximum(i - n_lat_tiles, 0), 0))]
        x_args = (x_lat, x_ctx)
    else:
        x_specs = [pl.BlockSpec((TM, d), lambda i: (i, 0))]
        x_args = (x_lat,)
    return pl.pallas_call(
        kern,
        grid=(n_rows // TM,),
        in_specs=x_specs + [pl.BlockSpec((TM, attn.shape[1]), lambda i: (i, 0)),
                            col(0), col(1), col(2), col(3), col(4),
                            prev(1), prev(2), nxt(1), nxt(2),
                            pl.BlockSpec((1, 6, d), lambda i: (seg_of_row_tile(i, TM), 0, 0)),
                            const((3, d)), const(wa.shape), const(wc.shape), const(wo.shape),
                            const((1, d)), const((1, d))],
        out_specs=pl.BlockSpec((TM, d), lambda i: (i, 0)),
        out_shape=jax.ShapeDtypeStruct((n_rows, d), F32),
        compiler_params=_cparams("parallel"),
        name="merge_ln1",
    )(*x_args, attn, proj, proj, proj, proj, proj, proj, proj, proj, proj, mod, conv_w, wa, wc, wo,
      g.reshape(1, d), b.reshape(1, d))


def _router_kernel(x_ref, mod_ref, rwt_ref, rb_ref, e_ref, w_ref, rank_ref, cnt_ref, cnt_scr, *, n_groups):
    i = pl.program_id(0)
    tm = x_ref.shape[0]
    n_exp = rwt_ref.shape[0]
    per = n_exp // n_groups

    @pl.when(i == 0)
    def _():
        cnt_scr[...] = jnp.zeros_like(cnt_scr)

    h2 = x_ref[...] * (1.0 + mod_ref[0, 4:5, :]) + mod_ref[0, 3:4, :]
    logits = lax.dot_general(rwt_ref[...].astype(BF16), h2.astype(BF16), NT_DIMS,
                             preferred_element_type=F32)
    ex = jnp.exp(logits - jnp.max(logits, axis=0, keepdims=True))
    probs = ex / jnp.sum(ex, axis=0, keepdims=True)
    sel3 = (probs + rb_ref[...]).reshape(n_groups, per, tm)

    li = lax.broadcasted_iota(jnp.int32, (n_groups, per, tm), 1).astype(F32)
    m1 = jnp.max(sel3, axis=1, keepdims=True)
    i1 = jnp.min(jnp.where(sel3 == m1, li, float(per)), axis=1, keepdims=True)
    sel3b = jnp.where(li == i1, -jnp.inf, sel3)
    m2 = jnp.max(sel3b, axis=1, keepdims=True)
    i2 = jnp.min(jnp.where(sel3b == m2, li, float(per)), axis=1, keepdims=True)
    score = m1 + m2
    gi = lax.broadcasted_iota(jnp.int32, (n_groups, 1, tm), 0).astype(F32)
    gbest = jnp.min(jnp.where(score == jnp.max(score, axis=0, keepdims=True), gi, float(n_groups)),
                    axis=0, keepdims=True)
    pick = gi == gbest
    l1 = jnp.sum(jnp.where(pick, i1, 0.0), axis=0)
    l2 = jnp.sum(jnp.where(pick, i2, 0.0), axis=0)
    e1 = gbest[0] * per + l1
    e2 = gbest[0] * per + l2

    ei = lax.broadcasted_iota(jnp.int32, (n_exp, tm), 0).astype(F32)
    is1 = ei == e1
    is2 = ei == e2
    p1 = jnp.sum(jnp.where(is1, probs, 0.0), axis=0, keepdims=True)
    p2 = jnp.sum(jnp.where(is2, probs, 0.0), axis=0, keepdims=True)
    psum = p1 + p2

    onehot = jnp.where(jnp.logical_or(is1, is2), 1.0, 0.0)
    srow = lax.broadcasted_iota(jnp.int32, (tm, tm), 0)
    scol = lax.broadcasted_iota(jnp.int32, (tm, tm), 1)
    before = jnp.where(srow < scol, 1.0, 0.0).astype(BF16)
    excl = jnp.dot(onehot.astype(BF16), before, preferred_element_type=F32) + cnt_scr[:, 0:1]
    r1 = jnp.sum(jnp.where(is1, excl, 0.0), axis=0, keepdims=True)
    r2 = jnp.sum(jnp.where(is2, excl, 0.0), axis=0, keepdims=True)
    cnt_scr[...] = cnt_scr[...] + jnp.sum(onehot, axis=1, keepdims=True)

    e_ref[0:1, :] = e1.astype(jnp.int32)
    e_ref[1:2, :] = e2.astype(jnp.int32)
    w_ref[0:1, :] = p1 / psum
    w_ref[1:2, :] = p2 / psum
    rank_ref[0:1, :] = r1.astype(jnp.int32)
    rank_ref[1:2, :] = r2.astype(jnp.int32)
    cnt_ref[...] = cnt_scr[...]


def _router(x1, mod, rwt, rb, *, seg_of_row_tile, d):
    r = x1.shape[0]
    n_exp = rwt.shape[0]
    row2 = lambda dt: jax.ShapeDtypeStruct((TOP_K, r), dt)
    spec2 = pl.BlockSpec((TOP_K, TM_ROUTER), lambda i: (0, i))
    assert r % TM_ROUTER == 0
    return pl.pallas_call(
        functools.partial(_router_kernel, n_groups=N_EXPERT_GROUPS),
        grid=(r // TM_ROUTER,),
        in_specs=[pl.BlockSpec((TM_ROUTER, d), lambda i: (i, 0)),
                  pl.BlockSpec((1, 6, d), lambda i: (seg_of_row_tile(i, TM_ROUTER), 0, 0)),
                  pl.BlockSpec((n_exp, d), lambda i: (0, 0)),
                  pl.BlockSpec((n_exp, 1), lambda i: (0, 0))],
        out_specs=[spec2, spec2, spec2, pl.BlockSpec((n_exp, LANE), lambda i: (0, 0))],
        out_shape=[row2(jnp.int32), row2(F32), row2(jnp.int32), jax.ShapeDtypeStruct((n_exp, LANE), F32)],
        scratch_shapes=[pltpu.VMEM((n_exp, LANE), F32)],
        compiler_params=_cparams("arbitrary"),
        name="router",
    )(x1, mod, rwt, rb.reshape(n_exp, 1))


def _dispatch_kernel(dest_ref, fill_lo_ref, fill_hi_ref, x_ref, mod_ref, xs_ref, buf0, buf1, zslab, sems, zsem, *,
                     n_rows):
    i = pl.program_id(0)
    n_steps = pl.num_programs(0)
    tm = x_ref.shape[0]
    bufs = (buf0, buf1)
    slot = i % 2
    words = _pack_rows(x_ref[...] * (1.0 + mod_ref[0, 4:5, :]) + mod_ref[0, 3:4, :])

    def drain(sl):
        for _ in range(TOP_K):
            pltpu.make_async_copy(bufs[sl], xs_ref.at[pl.ds(0, tm * SUBLANE)], sems.at[sl]).wait()

    def fill_and_issue(sl):
        for s, w in enumerate(words):
            bufs[sl][pl.ds(s, tm, stride=SUBLANE), :] = w

        def issue(t, carry):
            src = bufs[sl].at[pl.ds(pl.multiple_of(t * SUBLANE, SUBLANE), SUBLANE)]
            for k in range(TOP_K):
                dst = dest_ref[k * n_rows + i * tm + t]
                pltpu.make_async_copy(src, xs_ref.at[pl.ds(pl.multiple_of(dst * SUBLANE, SUBLANE), SUBLANE)],
                                      sems.at[sl]).start(priority=k % 2)
            return carry

        lax.fori_loop(0, tm, issue, 0, unroll=ISSUE_UNROLL)

    for sl in range(2):
        pl.when(slot == sl)(functools.partial(fill_and_issue, sl))
    for sl in range(2):
        pl.when(jnp.logical_and(i > 0, slot != sl))(functools.partial(drain, sl))

    @pl.when(i == n_steps - 1)
    def _():
        for sl in range(2):
            pl.when(slot == sl)(functools.partial(drain, sl))
        zslab[...] = jnp.zeros_like(zslab)
        n_ranges = fill_lo_ref.shape[0]

        def zero_copy(rr):
            return pltpu.make_async_copy(zslab, xs_ref.at[pl.ds(pl.multiple_of(rr * SUBLANE, SUBLANE), SUBLANE)], zsem)

        def per_range(fn):
            def body(e, carry):
                lax.fori_loop(fill_lo_ref[e], fill_hi_ref[e], lambda rr, c: (fn(rr), c)[1], 0)
                return carry
            lax.fori_loop(0, n_ranges, body, 0)

        per_range(lambda rr: zero_copy(rr).start())
        per_range(lambda rr: zero_copy(rr).wait())


def _dispatch(x1, mod, dest_flat, fill_lo, fill_hi, cap, *, seg_of_row_tile, d):
    r = x1.shape[0]
    slab = _slab_rows(d)
    grid_spec = pltpu.PrefetchScalarGridSpec(
        num_scalar_prefetch=3,
        grid=(r // TM,),
        in_specs=[pl.BlockSpec((TM, d), lambda i, *_: (i, 0)),
                  pl.BlockSpec((1, 6, d), lambda i, *_: (seg_of_row_tile(i, TM), 0, 0))],
        out_specs=pl.BlockSpec(memory_space=pl.ANY),
        scratch_shapes=[pltpu.VMEM((TM * slab, LANE), U32), pltpu.VMEM((TM * slab, LANE), U32),
                        pltpu.VMEM((slab, LANE), U32),
                        pltpu.SemaphoreType.DMA((2,)), pltpu.SemaphoreType.DMA(())],
    )
    return pl.pallas_call(
        functools.partial(_dispatch_kernel, n_rows=r),
        grid_spec=grid_spec,
        out_shape=jax.ShapeDtypeStruct((cap * slab, LANE), U32),
        compiler_params=_cparams("arbitrary"),
        name="moe_dispatch",
    )(dest_flat, fill_lo, fill_hi, x1, mod)


def _expert_kernel(blk_e_ref, n_used_ref, xs_ref, wg_ref, wu_ref, wd_ref, o_ref):
    b = pl.program_id(0)
    mb = xs_ref.shape[0] // SUBLANE

    @pl.when(b < n_used_ref[0])
    def _():
        xb = _unpack_rows([xs_ref[pl.ds(s, mb, stride=SUBLANE), :] for s in range(SUBLANE)]).astype(BF16)
        g = jnp.dot(xb, wg_ref[...], preferred_element_type=F32)
        u = jnp.dot(xb, wu_ref[...], preferred_element_type=F32)
        hmid = (g * jax.nn.sigmoid(g) * u).astype(BF16)
        y = jnp.dot(hmid, wd_ref[...], preferred_element_type=F32)
        for s, w in enumerate(_pack_rows(y)):
            o_ref[pl.ds(s, mb, stride=SUBLANE), :] = w

    @pl.when(b >= n_used_ref[0])
    def _():
        o_ref[...] = jnp.zeros_like(o_ref)


def _experts(xs, blk_expert, n_used, wg, wu, wd):
    d, ff = wg.shape[1], wg.shape[2]
    slab = _slab_rows(d)
    n_blk = xs.shape[0] // (MOE_BLOCK * slab)
    xrow = lambda b, be, nu: (jnp.minimum(b, nu[0] - 1), 0)
    grid_spec = pltpu.PrefetchScalarGridSpec(
        num_scalar_prefetch=2,
        grid=(n_blk,),
        in_specs=[pl.BlockSpec((MOE_BLOCK * slab, LANE), xrow),
                  pl.BlockSpec((None, d, ff), lambda b, be, nu: (be[b], 0, 0)),
                  pl.BlockSpec((None, d, ff), lambda b, be, nu: (be[b], 0, 0)),
                  pl.BlockSpec((None, ff, d), lambda b, be, nu: (be[b], 0, 0))],
        out_specs=pl.BlockSpec((MOE_BLOCK * slab, LANE), lambda b, be, nu: (b, 0)),
    )
    return pl.pallas_call(
        _expert_kernel,
        grid_spec=grid_spec,
        out_shape=jax.ShapeDtypeStruct(xs.shape, U32),
        compiler_params=_cparams("arbitrary"),
        name="moe_experts",
    )(blk_expert, n_used, xs, wg, wu, wd)


def _combine_kernel(*refs, n_rows, alpha, emit_h):
    bufs, sems = refs[-COMBINE_SLOTS - 1:-1], refs[-1]
    refs = refs[:-COMBINE_SLOTS - 1]
    if emit_h:
        dest_ref, x_ref, wt_ref, mod_ref, g_ref, b_ref, modn_ref, ys_ref, o_ref, h_ref = refs
    else:
        dest_ref, x_ref, wt_ref, mod_ref, g_ref, b_ref, ys_ref, o_ref = refs
    i = pl.program_id(0)
    n_steps = pl.num_programs(0)
    tm = x_ref.shape[0]
    n_slots = len(bufs)
    ahead = n_slots - 1
    slot = i % n_slots

    def gather(tile, sl, t, k):
        src = dest_ref[k * n_rows + tile * tm + t]
        row0 = t * SUBLANE if isinstance(t, int) else pl.multiple_of(t * SUBLANE, SUBLANE)
        return pltpu.make_async_copy(ys_ref.at[pl.ds(pl.multiple_of(src * SUBLANE, SUBLANE), SUBLANE)],
                                     bufs[sl].at[k, pl.ds(row0, SUBLANE)], sems.at[sl])

    def wait_tile(sl):
        for k in range(TOP_K):
            pltpu.make_async_copy(ys_ref.at[pl.ds(0, tm * SUBLANE)], bufs[sl].at[k], sems.at[sl]).wait()

    @pl.when(i == 0)
    def _():
        for j in range(ahead):
            def body(t, carry, j=j):
                for k in range(TOP_K):
                    gather(jnp.minimum(j, n_steps - 1), j, t, k).start(priority=k % 2)
                return carry
            lax.fori_loop(0, tm, body, 0, unroll=ISSUE_UNROLL)

    def finish(sl):
        wait_tile(sl)
        nxt = jnp.minimum(i + ahead, n_steps - 1)
        for t in range(tm):
            for k in range(TOP_K):
                gather(nxt, (sl + ahead) % n_slots, t, k).start(priority=k % 2)
        y = [_unpack_rows([bufs[sl][k, pl.ds(s, tm, stride=SUBLANE), :] for s in range(SUBLANE)])
             for k in range(TOP_K)]
        f = wt_ref[:, 0:1] * y[0] + wt_ref[:, 1:2] * y[1]
        z = alpha * x_ref[...] + mod_ref[0, 5:6, :] * f
        x2 = _layer_norm(z, g_ref[...], b_ref[...])
        o_ref[...] = x2
        if emit_h:
            h_ref[...] = (x2 * (1.0 + modn_ref[0, 1:2, :]) + modn_ref[0, 0:1, :]).astype(BF16)

    for sl in range(n_slots):
        pl.when(slot == sl)(functools.partial(finish, sl))

    for sl in range(n_slots):
        pl.when(jnp.logical_and(i == n_steps - 1, slot != sl))(functools.partial(wait_tile, sl))


def _combine(x1, wt_rows, mod, g, b, dest_flat, ys, mod_next, *, seg_of_row_tile, d, alpha):
    r = x1.shape[0]
    slab = _slab_rows(d)
    emit_h = mod_next is not None
    mod_spec = pl.BlockSpec((1, 6, d), lambda i, *_: (seg_of_row_tile(i, TM), 0, 0))
    row_spec = pl.BlockSpec((TM, d), lambda i, *_: (i, 0))
    vec_spec = pl.BlockSpec((1, d), lambda i, *_: (0, 0))
    in_specs = [row_spec, pl.BlockSpec((TM, TOP_K), lambda i, *_: (i, 0)), mod_spec, vec_spec, vec_spec]
    args = [x1, wt_rows, mod, g.reshape(1, d), b.reshape(1, d)]
    out_specs, out_shape = [row_spec], [jax.ShapeDtypeStruct((r, d), F32)]
    if emit_h:
        in_specs.append(mod_spec)
        args.append(mod_next)
        out_specs.append(row_spec)
        out_shape.append(jax.ShapeDtypeStruct((r, d), BF16))
    in_specs.append(pl.BlockSpec(memory_space=pl.ANY))
    args.append(ys)
    grid_spec = pltpu.PrefetchScalarGridSpec(
        num_scalar_prefetch=1,
        grid=(r // TM,),
        in_specs=in_specs,
        out_specs=out_specs,
        scratch_shapes=[pltpu.VMEM((TOP_K, TM * slab, LANE), U32) for _ in range(COMBINE_SLOTS)]
        + [pltpu.SemaphoreType.DMA((COMBINE_SLOTS,))],
    )
    outs = pl.pallas_call(
        functools.partial(_combine_kernel, n_rows=r, alpha=alpha, emit_h=emit_h),
        grid_spec=grid_spec,
        out_shape=out_shape,
        compiler_params=_cparams("arbitrary"),
        name="moe_combine_ln2",
    )(dest_flat, *args)
    return (outs[0], outs[1]) if emit_h else (outs[0], None)


def _rope_tables(n_batch, n_seq, ctx_rows):
    nf = HEAD_DIM // 4
    inv_freq = np.power(np.float32(ROPE_BASE), -np.arange(nf, dtype=np.float32) / np.float32(nf)).astype(np.float32)
    n_grid_rows = n_seq // GRID_W
    ar = (np.arange(n_grid_rows, dtype=np.float32)[:, None] * inv_freq).astype(np.float32)
    ac = (np.arange(GRID_W, dtype=np.float32)[:, None] * inv_freq).astype(np.float32)
    shape = (n_grid_rows, GRID_W, nf)
    by_row = lambda a: jnp.broadcast_to(jnp.asarray(a, F32)[:, None, :], shape)
    by_col = lambda a: jnp.broadcast_to(jnp.asarray(a, F32)[None, :, :], shape)
    cr, sr, cc, sc = np.cos(ar), np.sin(ar), np.cos(ac), np.sin(ac)
    cos_t = jnp.concatenate([by_row(cr), by_row(cr), by_col(cc), by_col(cc)], axis=-1).reshape(n_seq, HEAD_DIM)
    sin_t = jnp.concatenate([by_row(-sr), by_row(sr), by_col(-sc), by_col(sc)], axis=-1).reshape(n_seq, HEAD_DIM)
    cos_t = jnp.concatenate([cos_t] * n_batch + [jnp.ones((ctx_rows, HEAD_DIM), F32)], axis=0)
    sin_t = jnp.concatenate([sin_t] * n_batch + [jnp.zeros((ctx_rows, HEAD_DIM), F32)], axis=0)
    return cos_t, sin_t


def kernel(x, c, ctx, c_ctx, w_ada, b_ada, w_in, attn_sink, conv_w, w_attn_proj, w_conv_proj, w_out,
           ln1_g, ln1_b, ln2_g, ln2_b, router_w, router_b, w_gate, w_up, w_down):
    n_batch, n_seq, d = x.shape
    n_ctx = ctx.shape[1]
    depth = w_ada.shape[0]
    n_exp = router_w.shape[1]
    q_width = attn_sink.shape[1] * HEAD_DIM
    in_width = w_in.shape[2]
    kv_width = (in_width - q_width - 5 * d) // 2
    t = n_batch * n_seq
    r = t + n_batch * n_ctx
    alpha = float((2 * depth) ** 0.25)
    assert w_conv_proj.shape[1] == d and n_seq % GRID_W == 0
    assert r % TM_IN == 0 and r % TM == 0 and r % TQ == 0

    def seg_of_row_tile(i, tile):
        return jnp.minimum((i * tile) // n_seq, n_batch)

    x_lat, x_ctx = x.reshape(t, d), ctx.reshape(n_batch * n_ctx, d)
    rope_c, rope_s = _rope_tables(n_batch, n_seq, n_batch * n_ctx)
    c_rows = jnp.concatenate([c, c_ctx[None, :], jnp.zeros((8 - n_batch - 1, d), F32)], axis=0)
    mods = [_mod_table(c_rows, w_ada, b_ada[l], l).reshape(8, 6, d) for l in range(depth)]
    rwt = router_w.T
    q_col, k_col, v_col = 5 * d, 5 * d + q_width, 5 * d + q_width + kv_width
    expert_ids = jnp.arange(n_exp, dtype=jnp.int32)

    h = _modulate(x_lat, x_ctx, mods[0], seg_of_row_tile=seg_of_row_tile)
    x_cur = None
    for l in range(depth):
        last = l == depth - 1
        mod = mods[l]
        n_rows = t if last else r
        proj, wg_b, wu_b, wd_b = _in_proj(h, w_in, l, rope_c, rope_s, w_gate, w_up, w_down, d=d, q_width=q_width,
                                          kv_width=kv_width)
        attn = _attention(proj, attn_sink[l], n_rows=n_rows, n_batch=n_batch, n_seq=n_seq, n_ctx=n_ctx, q_col=q_col,
                          k_col=k_col, v_col=v_col, q_width=q_width, kv_width=kv_width)
        src = (x_lat, x_ctx) if l == 0 else (x_cur, None)
        x1 = _merge(src[0], src[1], attn, proj, mod, conv_w[l], w_attn_proj[l].astype(BF16),
                    w_conv_proj[l].astype(BF16), w_out[l].astype(BF16), ln1_g[l], ln1_b[l], n_rows=n_rows,
                    seg_of_row_tile=seg_of_row_tile, d=d, n_seq=n_seq, n_ctx=n_ctx, n_batch=n_batch, alpha=alpha)

        e_idx, wts, rank, cnt = _router(x1, mod, rwt, router_b, seg_of_row_tile=seg_of_row_tile, d=d)
        cap = (n_rows * TOP_K + n_exp * (MOE_BLOCK - 1) + MOE_BLOCK - 1) // MOE_BLOCK * MOE_BLOCK
        n_blk = cap // MOE_BLOCK
        counts = cnt[:, 0].astype(jnp.int32)
        padded = (counts + MOE_BLOCK - 1) // MOE_BLOCK * MOE_BLOCK
        pend = jnp.cumsum(padded)
        pstart = pend - padded
        slot0 = jnp.sum(jnp.where(e_idx[:, :, None] == expert_ids, pstart, 0), axis=-1)
        dest_flat = (slot0 + rank).reshape(-1)
        n_used = (pend[-1] // MOE_BLOCK).astype(jnp.int32)
        blk_start = jnp.minimum(jnp.arange(n_blk, dtype=jnp.int32), n_used - 1) * MOE_BLOCK
        blk_expert = jnp.sum(blk_start[:, None] >= pend[None, :], axis=-1).astype(jnp.int32)
        fill_lo = jnp.concatenate([pstart + counts, pend[-1:]]).astype(jnp.int32)
        fill_hi = jnp.concatenate([pend, jnp.full((1,), cap, pend.dtype)]).astype(jnp.int32)
        xs = _dispatch(x1, mod, dest_flat, fill_lo, fill_hi, cap, seg_of_row_tile=seg_of_row_tile, d=d)
        ys = _experts(xs, blk_expert, n_used.reshape(1), wg_b, wu_b, wd_b)
        x_cur, h = _combine(x1, wts.T, mod, ln2_g[l], ln2_b[l], dest_flat, ys, None if last else mods[l + 1],
                            seg_of_row_tile=seg_of_row_tile, d=d, alpha=alpha)

    return x_cur.reshape(n_batch, n_seq, d)
```
